```python
import math
import jax
import jax.numpy as jnp
from jax import lax
import numpy as np

D_MODEL = 2048
BATCH = 4
SEQ = 4096
DEPTH = 4
DEC_BATCH = 16
DEC_SEQ = 64
PAST_LEN = 1024

CHUNK = 64
HEAD_DIM = 64
A_HEADS = 16
A_KV_HEADS = 4
A_WINDOW = 128
A_BACK_CHUNKS = -(-(A_WINDOW - 1) // CHUNK)
B_HEADS = 16
B_BACK_CHUNKS = 8
B_REL_CLIP = 128
T5_BUCKETS = 32
T5_MAX_DIST = 128
PLE_DIM = 256
D_FF = ((8 * D_MODEL + 767) // 768) * 256
EPS = 1e-6
NEG_INF = -1e30

A_Q = A_HEADS * HEAD_DIM
A_KV = A_KV_HEADS * HEAD_DIM
B_W = B_HEADS * HEAD_DIM
IN_SPLITS = [A_Q, A_Q + A_KV, A_Q + 2 * A_KV, A_Q + 2 * A_KV + B_W,
             A_Q + 2 * A_KV + 2 * B_W, A_Q + 2 * A_KV + 3 * B_W,
             A_Q + 2 * A_KV + 3 * B_W + D_MODEL]
IN_WIDTH = A_Q + 2 * A_KV + 3 * B_W + 2 * D_MODEL

kernel_name = "hybrid_streaming_encoder_step"


def rms_norm(x, g):
    xf = x.astype(jnp.float32)
    y = xf * lax.rsqrt(jnp.mean(xf * xf, axis=-1, keepdims=True) + EPS)
    return (y * g.astype(jnp.float32)).astype(x.dtype)


def t5_bucket(rel):
    half = T5_BUCKETS // 2
    max_exact = half // 2
    ret = jnp.where(rel > 0, half, 0)
    n = jnp.abs(rel)
    nf = jnp.maximum(n, 1).astype(jnp.float32)
    large = max_exact + (jnp.log(nf / max_exact) / math.log(T5_MAX_DIST / max_exact)
                         * (half - max_exact)).astype(jnp.int32)
    large = jnp.minimum(large, half - 1)
    return ret + jnp.where(n < max_exact, n, large)


def t5_bias(table, qpos, kpos):
    return jnp.moveaxis(table[t5_bucket(kpos[None, :] - qpos[:, None])], -1, 0)


def clipped_rel_bias(table, qpos, kpos):
    idx = jnp.clip(qpos[:, None] - kpos[None, :], -B_REL_CLIP, B_REL_CLIP) + B_REL_CLIP
    return jnp.moveaxis(table[idx], -1, 0)


def band_attention(q, k, v, qpos, kpos, bias, n_back, sink=None):
    b, lq, h, dh = q.shape
    g = k.shape[2]
    r = h // g
    qg = q.reshape(b, lq, g, r, dh)
    s = jnp.einsum('bqgrd,bkgd->bgrqk', qg, k).astype(jnp.float32) * (dh ** -0.5)
    s = s + bias.reshape(g, r, lq, -1).astype(jnp.float32)
    qc = qpos[:, None] // CHUNK
    kc = kpos[None, :] // CHUNK
    mask = (kpos[None, :] >= 0) & (kc <= qc) & (kc >= qc - n_back)
    s = jnp.where(mask, s, NEG_INF)
    if sink is None:
        p = jax.nn.softmax(s, axis=-1)
    else:
        sk = sink.astype(jnp.float32).reshape(g, r, 1, 1)
        m = jnp.maximum(jnp.max(s, axis=-1, keepdims=True), sk)
        e = jnp.exp(s - m)
        p = e / (jnp.sum(e, axis=-1, keepdims=True) + jnp.exp(sk - m))
    o = jnp.einsum('bgrqk,bkgd->bqgrd', p.astype(v.dtype), v)
    return o.reshape(b, lq, h * dh)


def prompt_band_attention(q, k, v, n_back, bias_fn, sink=None):
    b, s, h, dh = q.shape
    band = (n_back + 1) * CHUNK
    pad = n_back * CHUNK
    kp = jnp.pad(k, ((0, 0), (pad, 0), (0, 0), (0, 0)))
    vp = jnp.pad(v, ((0, 0), (pad, 0), (0, 0), (0, 0)))

    def one_chunk(c):
        q0 = c * CHUNK
        qc = lax.dynamic_slice_in_dim(q, q0, CHUNK, axis=1)
        kc = lax.dynamic_slice_in_dim(kp, q0, band, axis=1)
        vc = lax.dynamic_slice_in_dim(vp, q0, band, axis=1)
        qpos = q0 + jnp.arange(CHUNK, dtype=jnp.int32)
        kpos = q0 - pad + jnp.arange(band, dtype=jnp.int32)
        return band_attention(qc, kc, vc, qpos, kpos, bias_fn(qpos, kpos), n_back, sink)

    out = lax.map(one_chunk, jnp.arange(s // CHUNK, dtype=jnp.int32))
    return jnp.swapaxes(out, 0, 1).reshape(b, s, h * dh)


def sample_band_attention(q, k_new, v_new, k_cache, v_cache, n_back, bias_fn, sink=None):
    lq = q.shape[1]
    lc = k_cache.shape[1]
    k = jnp.concatenate([k_cache, k_new.astype(k_cache.dtype)], axis=1)
    v = jnp.concatenate([v_cache, v_new.astype(v_cache.dtype)], axis=1)
    qpos = PAST_LEN + jnp.arange(lq, dtype=jnp.int32)
    kpos = jnp.concatenate([PAST_LEN - lc + jnp.arange(lc, dtype=jnp.int32), qpos])
    return band_attention(q, k, v, qpos, kpos, bias_fn(qpos, kpos), n_back, sink)


def mixer_projections(h, w_in_i):
    b, s, _ = h.shape
    z = h @ w_in_i
    qa, ka, va, qb, kb, vb, ga, gb = jnp.split(z, IN_SPLITS, axis=-1)
    return (qa.reshape(b, s, A_HEADS, HEAD_DIM), ka.reshape(b, s, A_KV_HEADS, HEAD_DIM),
            va.reshape(b, s, A_KV_HEADS, HEAD_DIM), qb.reshape(b, s, B_HEADS, HEAD_DIM),
            kb.reshape(b, s, B_HEADS, HEAD_DIM), vb.reshape(b, s, B_HEADS, HEAD_DIM), ga, gb)


def finish_layer(x, oa, ob, ga, gb, p_i, w_a_out_i, w_b_out_i, w_o_i, ffn_norm_i,
                 w_gate_i, w_up_i, w_down_i, w_ple_i, w_ple_gate_i):
    mixed = jax.nn.sigmoid(ga) * (oa @ w_a_out_i) + jax.nn.sigmoid(gb) * (ob @ w_b_out_i)
    x = x + mixed @ w_o_i
    h = rms_norm(x, ffn_norm_i)
    x = x + (jax.nn.silu(h @ w_gate_i) * (h @ w_up_i)) @ w_down_i
    x = x + jax.nn.sigmoid(x @ w_ple_gate_i) * (p_i @ w_ple_i)
    return x


def setup_inputs(seed: int = 0) -> dict:
    key = jax.random.key(seed)
    ks = jax.random.split(key, 24)
    f32 = jnp.float32

    def nrm(k, shape, scale):
        return jax.random.normal(k, shape, f32) * scale

    a_rows = min(A_BACK_CHUNKS * CHUNK, PAST_LEN)
    b_rows = min(B_BACK_CHUNKS * CHUNK, PAST_LEN)
    return {
        "x_prompt": nrm(ks[0], (BATCH, SEQ, D_MODEL), 1.0),
        "x_sample": nrm(ks[1], (DEC_BATCH, DEC_SEQ, D_MODEL), 1.0),
        "cache_a_k": nrm(ks[2], (DEPTH, DEC_BATCH, a_rows, A_KV_HEADS, HEAD_DIM), 1.0),
        "cache_a_v": nrm(ks[3], (DEPTH, DEC_BATCH, a_rows, A_KV_HEADS, HEAD_DIM), 1.0),
        "cache_b_k": nrm(ks[4], (DEPTH, DEC_BATCH, b_rows, B_HEADS, HEAD_DIM), 1.0),
        "cache_b_v": nrm(ks[5], (DEPTH, DEC_BATCH, b_rows, B_HEADS, HEAD_DIM), 1.0),
        "p_prompt": nrm(ks[6], (DEPTH, BATCH, SEQ, PLE_DIM), 1.0),
        "p_sample": nrm(ks[7], (DEPTH, DEC_BATCH, DEC_SEQ, PLE_DIM), 1.0),
        "attn_norm": 1.0 + nrm(ks[8], (DEPTH, D_MODEL), 0.01),
        "w_in": nrm(ks[9], (DEPTH, D_MODEL, IN_WIDTH), D_MODEL ** -0.5),
        "a_sink": nrm(ks[10], (DEPTH, A_HEADS), 0.5),
        "t5_table": nrm(ks[11], (T5_BUCKETS, A_HEADS), 0.1),
        "b_rel_table": nrm(ks[12], (DEPTH, 2 * B_REL_CLIP + 1, B_HEADS), 0.1),
        "w_a_out": nrm(ks[13], (DEPTH, A_Q, D_MODEL), A_Q ** -0.5),
        "w_b_out": nrm(ks[14], (DEPTH, B_W, D_MODEL), B_W ** -0.5),
        "w_o": nrm(ks[15], (DEPTH, D_MODEL, D_MODEL), D_MODEL ** -0.5),
        "ffn_norm": 1.0 + nrm(ks[16], (DEPTH, D_MODEL), 0.01),
        "w_gate": nrm(ks[17], (DEPTH, D_MODEL, D_FF), D_MODEL ** -0.5),
        "w_up": nrm(ks[18], (DEPTH, D_MODEL, D_FF), D_MODEL ** -0.5),
        "w_down": nrm(ks[19], (DEPTH, D_FF, D_MODEL), D_FF ** -0.5),
        "w_ple": nrm(ks[20], (DEPTH, PLE_DIM, D_MODEL), PLE_DIM ** -0.5),
        "w_ple_gate": nrm(ks[21], (DEPTH, D_MODEL, D_MODEL), D_MODEL ** -0.5),
        "final_norm": 1.0 + nrm(ks[22], (D_MODEL,), 0.01),
    }


def reference(x_prompt, x_sample, cache_a_k, cache_a_v, cache_b_k, cache_b_v, p_prompt, p_sample,
              attn_norm, w_in, a_sink, t5_table, b_rel_table, w_a_out, w_b_out, w_o, ffn_norm,
              w_gate, w_up, w_down, w_ple, w_ple_gate, final_norm):
    seq = x_prompt.shape[1]
    a_keep = min(A_BACK_CHUNKS * CHUNK, seq)
    b_keep = min(B_BACK_CHUNKS * CHUNK, seq)
    xp, xs = x_prompt, x_sample
    akp, avp, bkp, bvp = [], [], [], []
    aks, avs, bks, bvs = [], [], [], []

    def t5_fn(qp, kp):
        return t5_bias(t5_table, qp, kp)

    for i in range(DEPTH):
        def rel_fn(qp, kp, tb=b_rel_table[i]):
            return clipped_rel_bias(tb, qp, kp)

        h = rms_norm(xp, attn_norm[i])
        qa, ka, va, qb, kb, vb, ga, gb = mixer_projections(h, w_in[i])
        oa = prompt_band_attention(qa, ka, va, A_BACK_CHUNKS, t5_fn, a_sink[i])
        ob = prompt_band_attention(qb, kb, vb, B_BACK_CHUNKS, rel_fn)
        xp = finish_layer(xp, oa, ob, ga, gb, p_prompt[i], w_a_out[i], w_b_out[i], w_o[i],
                          ffn_norm[i], w_gate[i], w_up[i], w_down[i], w_ple[i], w_ple_gate[i])
        akp.append(ka[:, seq - a_keep:])
        avp.append(va[:, seq - a_keep:])
        bkp.append(kb[:, seq - b_keep:])
        bvp.append(vb[:, seq - b_keep:])

        h = rms_norm(xs, attn_norm[i])
        qa, ka, va, qb, kb, vb, ga, gb = mixer_projections(h, w_in[i])
        oa = sample_band_attention(qa, ka, va, cache_a_k[i], cache_a_v[i], A_BACK_CHUNKS,
                                   t5_fn, a_sink[i])
        ob = sample_band_attention(qb, kb, vb, cache_b_k[i], cache_b_v[i], B_BACK_CHUNKS, rel_fn)
        xs = finish_layer(xs, oa, ob, ga, gb, p_sample[i], w_a_out[i], w_b_out[i], w_o[i],
                          ffn_norm[i], w_gate[i], w_up[i], w_down[i], w_ple[i], w_ple_gate[i])
        aks.append(ka)
        avs.append(va)
        bks.append(kb)
        bvs.append(vb)

    y_prompt = rms_norm(xp, final_norm)
    y_sample = rms_norm(xs, final_norm)
    return (y_prompt, y_sample,
            jnp.stack(akp, 0), jnp.stack(avp, 0), jnp.stack(bkp, 0), jnp.stack(bvp, 0),
            jnp.stack(aks, 0), jnp.stack(avs, 0), jnp.stack(bks, 0), jnp.stack(bvs, 0))
```

```python
import functools
import math

import jax
import jax.numpy as jnp
from jax import lax
from jax.experimental import pallas as pl
from jax.experimental.pallas import tpu as pltpu

D_MODEL = 2048
CHUNK = 64
HEAD_DIM = 64
A_HEADS = 16
A_KV_HEADS = 4
A_BACK_CHUNKS = 2
B_HEADS = 16
B_BACK_CHUNKS = 8
B_REL_CLIP = 128
T5_BUCKETS = 32
T5_MAX_DIST = 128
PLE_DIM = 256
EPS = 1e-6
NEG_INF = -1e30

A_Q = A_HEADS * HEAD_DIM
A_KV = A_KV_HEADS * HEAD_DIM
B_W = B_HEADS * HEAD_DIM
A_BAND = (A_BACK_CHUNKS + 1) * CHUNK
B_BAND = (B_BACK_CHUNKS + 1) * CHUNK
GROUP = 4
GROUP_W = GROUP * HEAD_DIM
N_GROUPS = A_HEADS // GROUP
KV_W = 2 * B_W + 2 * A_KV
REST_W = A_Q + B_W + 2 * D_MODEL
Q_BLOCK = B_BACK_CHUNKS * CHUNK

VMEM_LIMIT = 56 * 1024 * 1024

F32 = jnp.float32
BF16 = jnp.bfloat16


def _params(*sem):
    return pltpu.CompilerParams(dimension_semantics=sem, vmem_limit_bytes=VMEM_LIMIT)


def _resident(shape):
    zeros = (0,) * len(shape)
    return pl.BlockSpec(shape, lambda *_: zeros, pipeline_mode=pl.Buffered(1))


def _rms(x, g):
    y = x * lax.rsqrt(jnp.mean(x * x, axis=-1, keepdims=True) + EPS)
    return y * g


def _bias_kernel(idx_ref, tab_ref, out_ref):
    t = tab_ref[...]
    hi = t.astype(BF16)
    r1 = t - hi.astype(F32)
    mid = r1.astype(BF16)
    lo = (r1 - mid.astype(F32)).astype(BF16)
    k = t.shape[1]
    n = idx_ref.shape[1]
    onehot = (lax.broadcasted_iota(jnp.int32, (k, n), 0) == idx_ref[...]).astype(BF16)
    dot = functools.partial(jnp.dot, preferred_element_type=F32)
    out_ref[...] = dot(hi, onehot) + (dot(mid, onehot) + dot(lo, onehot))


def _bias_lookup(idx, table, band):
    entries, heads = table.shape
    k = -(-entries // 128) * 128
    tab_t = jnp.pad(table.T, ((0, 0), (0, k - entries)))
    n = CHUNK * band
    tn = n // 8
    out = pl.pallas_call(
        _bias_kernel,
        grid=(n // tn,),
        in_specs=[pl.BlockSpec((1, tn), lambda j: (0, j)),
                  pl.BlockSpec((heads, k), lambda j: (0, 0))],
        out_specs=pl.BlockSpec((heads, tn), lambda j: (0, j)),
        out_shape=jax.ShapeDtypeStruct((heads, n), F32),
        compiler_params=_params("arbitrary"),
        name="bias_lookup",
    )(idx.reshape(1, n), tab_t)
    return out.reshape(N_GROUPS, GROUP * CHUNK, band)


def _t5_bucket(rel):
    half = T5_BUCKETS // 2
    max_exact = half // 2
    ret = jnp.where(rel > 0, half, 0)
    n = jnp.abs(rel)
    nf = jnp.maximum(n, 1).astype(jnp.float32)
    large = max_exact + (jnp.log(nf / max_exact) / math.log(T5_MAX_DIST / max_exact)
                         * (half - max_exact)).astype(jnp.int32)
    large = jnp.minimum(large, half - 1)
    return ret + jnp.where(n < max_exact, n, large)


def _band_rel(band):
    qpos = jnp.arange(CHUNK, dtype=jnp.int32)
    kpos = jnp.arange(band, dtype=jnp.int32) - (band - CHUNK)
    return kpos[None, :] - qpos[:, None]


def _norm_kernel(x_ref, g_ref, h_ref):
    h_ref[...] = _rms(x_ref[...], g_ref[...]).astype(h_ref.dtype)


def _norm(x, g, tm=512):
    t = x.shape[0]
    return pl.pallas_call(
        _norm_kernel,
        grid=(t // tm,),
        in_specs=[pl.BlockSpec((tm, D_MODEL), lambda m: (m, 0)),
                  pl.BlockSpec((1, D_MODEL), lambda m: (0, 0))],
        out_specs=pl.BlockSpec((tm, D_MODEL), lambda m: (m, 0)),
        out_shape=jax.ShapeDtypeStruct((t, D_MODEL), BF16),
        compiler_params=_params("arbitrary"),
        name="rms_norm",
    )(x, g.reshape(1, D_MODEL))


def _kv_proj_kernel(h_ref, w_ref, sel_ref, kv32_ref, kvb_ref, kva_ref):
    r = jnp.dot(h_ref[...], w_ref[...], preferred_element_type=F32)
    kv32_ref[...] = r
    rb = r.astype(BF16)
    kvb_ref[...] = rb[:, :2 * B_W]
    ka = rb[:, 2 * B_W:2 * B_W + A_KV]
    va = rb[:, 2 * B_W + A_KV:]
    sel = sel_ref[...]
    kva_ref[:, :A_Q] = jnp.dot(ka, sel, preferred_element_type=F32).astype(BF16)
    kva_ref[:, A_Q:] = jnp.dot(va, sel, preferred_element_type=F32).astype(BF16)


def _kv_proj(h, w_kv, sel, tm=512):
    t = h.shape[0]
    return pl.pallas_call(
        _kv_proj_kernel,
        grid=(t // tm,),
        in_specs=[pl.BlockSpec((tm, D_MODEL), lambda m: (m, 0)),
                  _resident((D_MODEL, KV_W)),
                  _resident((A_KV, A_Q))],
        out_specs=[pl.BlockSpec((tm, KV_W), lambda m: (m, 0)),
                   pl.BlockSpec((tm, 2 * B_W), lambda m: (m, 0)),
                   pl.BlockSpec((tm, 2 * A_Q), lambda m: (m, 0))],
        out_shape=[jax.ShapeDtypeStruct((t, KV_W), F32),
                   jax.ShapeDtypeStruct((t, 2 * B_W), BF16),
                   jax.ShapeDtypeStruct((t, 2 * A_Q), BF16)],
        compiler_params=_params("arbitrary"),
        name="kv_proj",
    )(h, w_kv, sel)


def _matmul_kernel(h_ref, w_ref, o_ref):
    o_ref[...] = jnp.dot(h_ref[...], w_ref[...], preferred_element_type=F32).astype(o_ref.dtype)


def _rest_proj(h, w_rest, tm=1024, tn=2048):
    t = h.shape[0]
    return pl.pallas_call(
        _matmul_kernel,
        grid=(REST_W // tn, t // tm),
        in_specs=[pl.BlockSpec((tm, D_MODEL), lambda n, m: (m, 0)),
                  pl.BlockSpec((D_MODEL, tn), lambda n, m: (0, n))],
        out_specs=pl.BlockSpec((tm, tn), lambda n, m: (m, n)),
        out_shape=jax.ShapeDtypeStruct((t, REST_W), BF16),
        compiler_params=_params("arbitrary", "arbitrary"),
        name="rest_proj",
    )(h, w_rest)


def _mixer(q, k_ref, v_ref, start, band, bias_ref, sink_ref, first_valid):
    lane_head = lax.broadcasted_iota(jnp.int32, (CHUNK, GROUP_W), 1) // HEAD_DIM
    col = lax.broadcasted_iota(jnp.int32, (GROUP * CHUNK, band), 1)
    valid = col >= first_valid
    outs = []
    for g in range(N_GROUPS):
        cols = slice(g * GROUP_W, (g + 1) * GROUP_W)
        q4 = q[:, cols] * (HEAD_DIM ** -0.5)
        lhs = jnp.concatenate(
            [jnp.where(lane_head == r, q4, jnp.zeros_like(q4)) for r in range(GROUP)], axis=0)
        k4 = k_ref[pl.ds(start, band), cols]
        s = lax.dot_general(lhs, k4, (((1,), (1,)), ((), ())), preferred_element_type=F32)
        s = s + bias_ref[g]
        s = jnp.where(valid, s, NEG_INF)
        m = jnp.max(s, axis=-1, keepdims=True)
        if sink_ref is not None:
            sk = sink_ref[g]
            m = jnp.maximum(m, sk)
        e = jnp.exp(s - m)
        l = jnp.sum(e, axis=-1, keepdims=True)
        if sink_ref is not None:
            l = l + jnp.exp(sk - m)
        v4 = v_ref[pl.ds(start, band), cols]
        pv = jnp.dot(e.astype(BF16), v4, preferred_element_type=F32) / l
        o4 = jnp.where(lane_head == 0, pv[:CHUNK], 0.0)
        for r in range(1, GROUP):
            o4 = o4 + jnp.where(lane_head == r, pv[r * CHUNK:(r + 1) * CHUNK], 0.0)
        outs.append(o4)
    return jnp.concatenate(outs, axis=1)


def _attn_prompt_kernel(qa_ref, qb_ref, kb_ref, vb_ref, ka_ref, va_ref,
                        bias_a_ref, bias_b_ref, sink_ref, oa_ref, ob_ref,
                        kb_buf, vb_buf, ka_buf, va_buf):
    i = pl.program_id(1)
    for src, buf in ((kb_ref, kb_buf), (vb_ref, vb_buf), (ka_ref, ka_buf), (va_ref, va_buf)):
        @pl.when(i == 0)
        def _():
            buf[:Q_BLOCK] = jnp.zeros((Q_BLOCK, buf.shape[1]), BF16)

        @pl.when(i > 0)
        def _():
            buf[:Q_BLOCK] = buf[Q_BLOCK:]

        buf[Q_BLOCK:] = src[...]

    def chunk(c, carry):
        r0 = pl.multiple_of(c * CHUNK, CHUNK)
        first_b = jnp.where(i == 0, (B_BACK_CHUNKS - c) * CHUNK, 0)
        first_a = jnp.where(i == 0, jnp.maximum(A_BACK_CHUNKS - c, 0) * CHUNK, 0)
        start_a = pl.multiple_of(r0 + (Q_BLOCK - A_BACK_CHUNKS * CHUNK), CHUNK)
        oa = _mixer(qa_ref[pl.ds(r0, CHUNK), :], ka_buf, va_buf, start_a, A_BAND,
                    bias_a_ref, sink_ref, first_a)
        oa_ref[pl.ds(r0, CHUNK), :] = oa.astype(oa_ref.dtype)
        ob = _mixer(qb_ref[pl.ds(r0, CHUNK), :], kb_buf, vb_buf, r0, B_BAND,
                    bias_b_ref, None, first_b)
        ob_ref[pl.ds(r0, CHUNK), :] = ob.astype(ob_ref.dtype)
        return carry

    lax.fori_loop(0, Q_BLOCK // CHUNK, chunk, 0)


def _attn_prompt(zr, kvb, kva, bias_a, bias_b, sink, batch, seq):
    t = zr.shape[0]
    nblk = seq // Q_BLOCK
    row = lambda b, i: b * nblk + i
    blk = lambda col: pl.BlockSpec((Q_BLOCK, A_Q), lambda b, i: (row(b, i), col))
    return pl.pallas_call(
        _attn_prompt_kernel,
        grid=(batch, nblk),
        in_specs=[blk(0), blk(1), blk(0), blk(1), blk(0), blk(1),
                  _resident(bias_a.shape), _resident(bias_b.shape), _resident(sink.shape)],
        out_specs=[blk(0), blk(0)],
        out_shape=[jax.ShapeDtypeStruct((t, A_Q), BF16), jax.ShapeDtypeStruct((t, B_W), BF16)],
        scratch_shapes=[pltpu.VMEM((2 * Q_BLOCK, A_Q), BF16) for _ in range(4)],
        compiler_params=_params("arbitrary", "arbitrary"),
        name="attn_prompt",
    )(zr, zr, kvb, kvb, kva, kva, bias_a, bias_b, sink)


def _attn_sample_kernel(qa_ref, qb_ref, kb_ref, vb_ref, ka_ref, va_ref,
                        bias_a_ref, bias_b_ref, sink_ref, oa_in, ob_in, oa_ref, ob_ref):
    del oa_in, ob_in
    oa = _mixer(qa_ref[...], ka_ref, va_ref, 0, A_BAND, bias_a_ref, sink_ref, 0)
    oa_ref[...] = oa.astype(oa_ref.dtype)
    ob = _mixer(qb_ref[...], kb_ref, vb_ref, 0, B_BAND, bias_b_ref, None, 0)
    ob_ref[...] = ob.astype(ob_ref.dtype)


def _attn_sample(zr, kb, vb, ka, va, bias_a, bias_b, sink, oa, ob, row0):
    nb = kb.shape[0]
    c0 = row0 // CHUNK
    qblk = lambda col: pl.BlockSpec((CHUNK, A_Q), lambda b: (c0 + b, col))
    band = lambda n: pl.BlockSpec((None, n, A_Q), lambda b: (b, 0, 0))
    return pl.pallas_call(
        _attn_sample_kernel,
        grid=(nb,),
        in_specs=[qblk(0), qblk(1), band(B_BAND), band(B_BAND), band(A_BAND), band(A_BAND),
                  _resident(bias_a.shape), _resident(bias_b.shape), _resident(sink.shape),
                  pl.BlockSpec(memory_space=pl.ANY), pl.BlockSpec(memory_space=pl.ANY)],
        out_specs=[qblk(0), qblk(0)],
        out_shape=[jax.ShapeDtypeStruct(oa.shape, oa.dtype), jax.ShapeDtypeStruct(ob.shape, ob.dtype)],
        input_output_aliases={9: 0, 10: 1},
        compiler_params=_params("arbitrary"),
        name="attn_sample",
    )(zr, zr, kb, vb, ka, va, bias_a, bias_b, sink, oa, ob)


def _merge_kernel(oa_ref, ob_ref, ga_ref, gb_ref, x_ref, wa_ref, wb_ref, wo_ref, o_ref):
    ta = jnp.dot(oa_ref[...], wa_ref[...], preferred_element_type=F32)
    tb = jnp.dot(ob_ref[...], wb_ref[...], preferred_element_type=F32)
    mixed = (jax.nn.sigmoid(ga_ref[...].astype(F32)) * ta
             + jax.nn.sigmoid(gb_ref[...].astype(F32)) * tb)
    o_ref[...] = x_ref[...] + jnp.dot(mixed.astype(BF16), wo_ref[...], preferred_element_type=F32)


def _merge(oa, ob, zr, x, wa, wb, wo, tm=256):
    t = x.shape[0]
    gate_blk = (A_Q + B_W) // D_MODEL
    return pl.pallas_call(
        _merge_kernel,
        grid=(t // tm,),
        in_specs=[pl.BlockSpec((tm, A_Q), lambda m: (m, 0)),
                  pl.BlockSpec((tm, B_W), lambda m: (m, 0)),
                  pl.BlockSpec((tm, D_MODEL), lambda m: (m, gate_blk)),
                  pl.BlockSpec((tm, D_MODEL), lambda m: (m, gate_blk + 1)),
                  pl.BlockSpec((tm, D_MODEL), lambda m: (m, 0)),
                  _resident(wa.shape), _resident(wb.shape), _resident(wo.shape)],
        out_specs=pl.BlockSpec((tm, D_MODEL), lambda m: (m, 0)),
        out_shape=jax.ShapeDtypeStruct((t, D_MODEL), F32),
        compiler_params=_params("arbitrary"),
        name="merge",
    )(oa, ob, zr, zr, x, wa, wb, wo)


def _ffn_kernel(x_ref, g_ref, wg_ref, wu_ref, wd_ref, o_ref, h_ref):
    f = pl.program_id(1)

    @pl.when(f == 0)
    def _():
        x = x_ref[...]
        h_ref[...] = _rms(x, g_ref[...]).astype(BF16)
        o_ref[...] = x

    h = h_ref[...]
    gate = jnp.dot(h, wg_ref[...], preferred_element_type=F32)
    up = jnp.dot(h, wu_ref[...], preferred_element_type=F32)
    act = (jax.nn.silu(gate) * up).astype(BF16)
    o_ref[...] += jnp.dot(act, wd_ref[...], preferred_element_type=F32)


def _ffn(x, g, wg, wu, wd, tm=1024, tf=512):
    t = x.shape[0]
    d_ff = wg.shape[1]
    return pl.pallas_call(
        _ffn_kernel,
        grid=(t // tm, d_ff // tf),
        in_specs=[pl.BlockSpec((tm, D_MODEL), lambda m, f: (m, 0), pipeline_mode=pl.Buffered(1)),
                  pl.BlockSpec((1, D_MODEL), lambda m, f: (0, 0)),
                  pl.BlockSpec((D_MODEL, tf), lambda m, f: (0, f)),
                  pl.BlockSpec((D_MODEL, tf), lambda m, f: (0, f)),
                  pl.BlockSpec((tf, D_MODEL), lambda m, f: (f, 0))],
        out_specs=pl.BlockSpec((tm, D_MODEL), lambda m, f: (m, 0)),
        out_shape=jax.ShapeDtypeStruct((t, D_MODEL), F32),
        scratch_shapes=[pltpu.VMEM((tm, D_MODEL), BF16)],
        compiler_params=_params("arbitrary", "arbitrary"),
        name="ffn",
    )(x, g.reshape(1, D_MODEL), wg, wu, wd)


def _ple_kernel(x_ref, p_ref, wpg_ref, wp_ref, g_ref, o_ref, h_ref):
    x = x_ref[...]
    gate = jax.nn.sigmoid(jnp.dot(x.astype(BF16), wpg_ref[...], preferred_element_type=F32))
    emb = jnp.dot(p_ref[...].astype(BF16), wp_ref[...], preferred_element_type=F32)
    y = x + gate * emb
    o_ref[...] = y
    h_ref[...] = _rms(y, g_ref[...]).astype(h_ref.dtype)


def _ple(x, p, wpg, wp, g_next, h_dtype, tm=512):
    t = x.shape[0]
    return pl.pallas_call(
        _ple_kernel,
        grid=(t // tm,),
        in_specs=[pl.BlockSpec((tm, D_MODEL), lambda m: (m, 0)),
                  pl.BlockSpec((tm, PLE_DIM), lambda m: (m, 0)),
                  _resident(wpg.shape), _resident(wp.shape),
                  pl.BlockSpec((1, D_MODEL), lambda m: (0, 0))],
        out_specs=[pl.BlockSpec((tm, D_MODEL), lambda m: (m, 0)),
                   pl.BlockSpec((tm, D_MODEL), lambda m: (m, 0))],
        out_shape=[jax.ShapeDtypeStruct((t, D_MODEL), F32),
                   jax.ShapeDtypeStruct((t, D_MODEL), h_dtype)],
        compiler_params=_params("arbitrary"),
        name="ple",
    )(x, p, wpg, wp, g_next.reshape(1, D_MODEL))


def kernel(x_prompt, x_sample, cache_a_k, cache_a_v, cache_b_k, cache_b_v, p_prompt, p_sample,
           attn_norm, w_in, a_sink, t5_table, b_rel_table, w_a_out, w_b_out, w_o, ffn_norm,
           w_gate, w_up, w_down, w_ple, w_ple_gate, final_norm):
    depth = w_in.shape[0]
    batch, seq, _ = x_prompt.shape
    dec_batch, dec_seq, _ = x_sample.shape
    tp = batch * seq
    ts = dec_batch * dec_seq
    assert dec_seq == CHUNK and seq % Q_BLOCK == 0
    assert cache_a_k.shape[2] == A_BACK_CHUNKS * CHUNK and cache_b_k.shape[2] == B_BACK_CHUNKS * CHUNK
    a_keep = min(A_BACK_CHUNKS * CHUNK, seq)
    b_keep = min(B_BACK_CHUNKS * CHUNK, seq)

    x = jnp.concatenate([x_prompt.reshape(tp, D_MODEL), x_sample.reshape(ts, D_MODEL)], axis=0)

    o_ka = A_Q
    o_va = o_ka + A_KV
    o_qb = o_va + A_KV
    o_kb = o_qb + B_W
    o_vb = o_kb + B_W
    o_ga = o_vb + B_W

    src = jnp.arange(A_KV, dtype=jnp.int32)
    dst = jnp.arange(A_Q, dtype=jnp.int32)
    sel = ((src[:, None] // HEAD_DIM == dst[None, :] // GROUP_W)
           & (src[:, None] % HEAD_DIM == dst[None, :] % HEAD_DIM)).astype(BF16)

    bias_a = _bias_lookup(_t5_bucket(_band_rel(A_BAND)), t5_table, A_BAND)
    idx_b = jnp.clip(-_band_rel(B_BAND), -B_REL_CLIP, B_REL_CLIP) + B_REL_CLIP

    h = _norm(x, attn_norm[0])
    caches = ([], [], [], [], [], [], [], [])
    for i in range(depth):
        wi = w_in[i]
        w_kv = jnp.concatenate([wi[:, o_kb:o_vb], wi[:, o_vb:o_ga], wi[:, o_ka:o_va], wi[:, o_va:o_qb]],
                               axis=1).astype(BF16)
        w_rest = jnp.concatenate([wi[:, :A_Q], wi[:, o_qb:o_kb], wi[:, o_ga:]], axis=1).astype(BF16)

        kv32, kvb, kva = _kv_proj(h, w_kv, sel)
        zr = _rest_proj(h, w_rest)

        bias_b = _bias_lookup(idx_b, b_rel_table[i], B_BAND)
        sink = jnp.repeat(a_sink[i], CHUNK).reshape(N_GROUPS, GROUP * CHUNK, 1)

        oa, ob = _attn_prompt(zr, kvb, kva, bias_a, bias_b, sink, batch, seq)

        def rep_a(c):
            b, rows = c.shape[:2]
            return jnp.broadcast_to(c[:, :, :, None, :], (b, rows, A_KV_HEADS, GROUP, HEAD_DIM)
                                    ).reshape(b, rows, A_Q).astype(BF16)

        new_b = kvb[tp:].reshape(dec_batch, dec_seq, 2 * B_W)
        new_a = kva[tp:].reshape(dec_batch, dec_seq, 2 * A_Q)
        kb_s = jnp.concatenate([cache_b_k[i].reshape(dec_batch, -1, B_W).astype(BF16), new_b[..., :B_W]], axis=1)
        vb_s = jnp.concatenate([cache_b_v[i].reshape(dec_batch, -1, B_W).astype(BF16), new_b[..., B_W:]], axis=1)
        ka_s = jnp.concatenate([rep_a(cache_a_k[i]), new_a[..., :A_Q]], axis=1)
        va_s = jnp.concatenate([rep_a(cache_a_v[i]), new_a[..., A_Q:]], axis=1)
        oa, ob = _attn_sample(zr, kb_s, vb_s, ka_s, va_s, bias_a, bias_b, sink, oa, ob, tp)

        x = _merge(oa, ob, zr, x, w_a_out[i].astype(BF16), w_b_out[i].astype(BF16), w_o[i].astype(BF16))
        x = _ffn(x, ffn_norm[i], w_gate[i].astype(BF16), w_up[i].astype(BF16), w_down[i].astype(BF16))
        p = jnp.concatenate([p_prompt[i].reshape(tp, PLE_DIM), p_sample[i].reshape(ts, PLE_DIM)], axis=0)
        last = i == depth - 1
        g_next = final_norm if last else attn_norm[i + 1]
        x, h = _ple(x, p, w_ple_gate[i].astype(BF16), w_ple[i].astype(BF16), g_next,
                    F32 if last else BF16)

        kvp = kv32[:tp].reshape(batch, seq, KV_W)
        kvs = kv32[tp:].reshape(dec_batch, dec_seq, KV_W)
        pieces = (
            kvp[:, seq - a_keep:, 2 * B_W:2 * B_W + A_KV].reshape(batch, a_keep, A_KV_HEADS, HEAD_DIM),
            kvp[:, seq - a_keep:, 2 * B_W + A_KV:].reshape(batch, a_keep, A_KV_HEADS, HEAD_DIM),
            kvp[:, seq - b_keep:, :B_W].reshape(batch, b_keep, B_HEADS, HEAD_DIM),
            kvp[:, seq - b_keep:, B_W:2 * B_W].reshape(batch, b_keep, B_HEADS, HEAD_DIM),
            kvs[..., 2 * B_W:2 * B_W + A_KV].reshape(dec_batch, dec_seq, A_KV_HEADS, HEAD_DIM),
            kvs[..., 2 * B_W + A_KV:].reshape(dec_batch, dec_seq, A_KV_HEADS, HEAD_DIM),
            kvs[..., :B_W].reshape(dec_batch, dec_seq, B_HEADS, HEAD_DIM),
            kvs[..., B_W:2 * B_W].reshape(dec_batch, dec_seq, B_HEADS, HEAD_DIM),
        )
        for lst, piece in zip(caches, pieces):
            lst.append(piece)

    y = h
    return (y[:tp].reshape(batch, seq, D_MODEL), y[tp:].reshape(dec_batch, dec_seq, D_MODEL),
            *[jnp.stack(lst, 0) for lst in caches])
```

```python
import functools
import math

import jax
import jax.numpy as jnp
from jax import lax
from jax.experimental import pallas as pl
from jax.experimental.pallas import tpu as pltpu

D_MODEL = 2048
CHUNK = 64
HEAD_DIM = 64
A_HEADS = 16
A_KV_HEADS = 4
A_BACK_CHUNKS = 2
B_HEADS = 16
B_BACK_CHUNKS = 8
B_REL_CLIP = 128
T5_BUCKETS = 32
T5_MAX_DIST = 128
PLE_DIM = 256
EPS = 1e-6
NEG_INF = -1e30

A_Q = A_HEADS * HEAD_DIM
A_KV = A_KV_HEADS * HEAD_DIM
B_W = B_HEADS * HEAD_DIM
A_BAND = (A_BACK_CHUNKS + 1) * CHUNK
B_BAND = (B_BACK_CHUNKS + 1) * CHUNK
GROUP = 4
GROUP_W = GROUP * HEAD_DIM
GROUP_ROWS = GROUP * CHUNK
N_GROUPS = A_HEADS // GROUP
ROW_BLOCK = 32
KV_W = 2 * B_W + 2 * A_KV
REST_W = A_Q + B_W + 2 * D_MODEL
Q_BLOCK = B_BACK_CHUNKS * CHUNK

VMEM_LIMIT = 56 * 1024 * 1024

F32 = jnp.float32
BF16 = jnp.bfloat16


def _params(*sem):
    return pltpu.CompilerParams(dimension_semantics=sem, vmem_limit_bytes=VMEM_LIMIT)


def _resident(shape):
    zeros = (0,) * len(shape)
    return pl.BlockSpec(shape, lambda *_: zeros, pipeline_mode=pl.Buffered(1))


def _rms(x, g):
    y = x * lax.rsqrt(jnp.mean(x * x, axis=-1, keepdims=True) + EPS)
    return y * g


def _bias_kernel(idx_ref, tab_ref, out_ref):
    t = tab_ref[...]
    hi = t.astype(BF16)
    r1 = t - hi.astype(F32)
    mid = r1.astype(BF16)
    lo = (r1 - mid.astype(F32)).astype(BF16)
    k = t.shape[1]
    n = idx_ref.shape[1]
    onehot = (lax.broadcasted_iota(jnp.int32, (k, n), 0) == idx_ref[...]).astype(BF16)
    dot = functools.partial(jnp.dot, preferred_element_type=F32)
    out_ref[...] = dot(hi, onehot) + (dot(mid, onehot) + dot(lo, onehot))


def _bias_lookup(idx, table, band):
    entries, heads = table.shape
    k = -(-entries // 128) * 128
    tab_t = jnp.pad(table.T, ((0, 0), (0, k - entries)))
    n = CHUNK * band
    tn = n // 8
    out = pl.pallas_call(
        _bias_kernel,
        grid=(n // tn,),
        in_specs=[pl.BlockSpec((1, tn), lambda j: (0, j)),
                  pl.BlockSpec((heads, k), lambda j: (0, 0))],
        out_specs=pl.BlockSpec((heads, tn), lambda j: (0, j)),
        out_shape=jax.ShapeDtypeStruct((heads, n), F32),
        compiler_params=_params("arbitrary"),
        name="bias_lookup",
    )(idx.reshape(1, n), tab_t)
    return out.reshape(N_GROUPS, GROUP_ROWS, band)


def _t5_bucket(rel):
    half = T5_BUCKETS // 2
    max_exact = half // 2
    ret = jnp.where(rel > 0, half, 0)
    n = jnp.abs(rel)
    nf = jnp.maximum(n, 1).astype(jnp.float32)
    large = max_exact + (jnp.log(nf / max_exact) / math.log(T5_MAX_DIST / max_exact)
                         * (half - max_exact)).astype(jnp.int32)
    large = jnp.minimum(large, half - 1)
    return ret + jnp.where(n < max_exact, n, large)


def _band_rel(band):
    qpos = jnp.arange(CHUNK, dtype=jnp.int32)
    kpos = jnp.arange(band, dtype=jnp.int32) - (band - CHUNK)
    return kpos[None, :] - qpos[:, None]


def _prompt_tile(m, n_p):
    return jnp.minimum(m, n_p - 1)


def _sample_tile(m, n_p):
    return jnp.maximum(m - n_p, 0)


def _norm0_kernel(xp_ref, xs_ref, g_ref, x_ref, h_ref, *, n_p):
    def emit(x):
        x_ref[...] = x
        h_ref[...] = _rms(x, g_ref[...]).astype(h_ref.dtype)

    m = pl.program_id(0)
    pl.when(m < n_p)(lambda: emit(xp_ref[...]))
    pl.when(m >= n_p)(lambda: emit(xs_ref[...]))


def _norm0(xp, xs, g, tm=512):
    tp, ts = xp.shape[0], xs.shape[0]
    n_p = tp // tm
    t = tp + ts
    return pl.pallas_call(
        functools.partial(_norm0_kernel, n_p=n_p),
        grid=(t // tm,),
        in_specs=[pl.BlockSpec((tm, D_MODEL), lambda m: (_prompt_tile(m, n_p), 0)),
                  pl.BlockSpec((tm, D_MODEL), lambda m: (_sample_tile(m, n_p), 0)),
                  pl.BlockSpec((1, D_MODEL), lambda m: (0, 0))],
        out_specs=[pl.BlockSpec((tm, D_MODEL), lambda m: (m, 0)),
                   pl.BlockSpec((tm, D_MODEL), lambda m: (m, 0))],
        out_shape=[jax.ShapeDtypeStruct((t, D_MODEL), F32),
                   jax.ShapeDtypeStruct((t, D_MODEL), BF16)],
        compiler_params=_params("arbitrary"),
        name="rms_norm",
    )(xp, xs, g.reshape(1, D_MODEL))


def _kv_proj_kernel(h_ref, w_ref, sel_ref, kvb_ref, kva_ref,
                    kbp_ref, vbp_ref, kap_ref, vap_ref, kbs_ref, vbs_ref, kas_ref, vas_ref,
                    *, n_p, tiles_per_seq):
    m = pl.program_id(0)
    r = jnp.dot(h_ref[...], w_ref[...], preferred_element_type=F32)
    rb = r.astype(BF16)
    kvb_ref[...] = rb[:, :2 * B_W]
    o_ka = 2 * B_W
    o_va = o_ka + A_KV
    sel = sel_ref[...]
    kva_ref[:, :A_Q] = jnp.dot(rb[:, o_ka:o_va], sel, preferred_element_type=F32).astype(BF16)
    kva_ref[:, A_Q:] = jnp.dot(rb[:, o_va:], sel, preferred_element_type=F32).astype(BF16)

    tm = r.shape[0]
    b_keep = kbp_ref.shape[0]
    a_keep = kap_ref.shape[0]

    @pl.when((m < n_p) & (m % tiles_per_seq == tiles_per_seq - 1))
    def _():
        kbp_ref[...] = r[tm - b_keep:, :B_W]
        vbp_ref[...] = r[tm - b_keep:, B_W:2 * B_W]
        kap_ref[...] = r[tm - a_keep:, o_ka:o_va]
        vap_ref[...] = r[tm - a_keep:, o_va:]

    @pl.when(m >= n_p)
    def _():
        kbs_ref[...] = r[:, :B_W]
        vbs_ref[...] = r[:, B_W:2 * B_W]
        kas_ref[...] = r[:, o_ka:o_va]
        vas_ref[...] = r[:, o_va:]


def _kv_proj(h, w_kv, sel, batch, seq, ts, a_keep, b_keep, tm=512):
    t = h.shape[0]
    n_p = batch * seq // tm
    tps = seq // tm
    assert seq % tm == 0 and ts % tm == 0 and a_keep <= tm and b_keep <= tm
    tail = lambda rows, w: pl.BlockSpec(
        (None, rows, w), lambda m: (jnp.minimum(m // tps, batch - 1), 0, 0))
    samp = lambda w: pl.BlockSpec((tm, w), lambda m: (_sample_tile(m, n_p), 0))
    return pl.pallas_call(
        functools.partial(_kv_proj_kernel, n_p=n_p, tiles_per_seq=tps),
        grid=(t // tm,),
        in_specs=[pl.BlockSpec((tm, D_MODEL), lambda m: (m, 0)),
                  _resident((D_MODEL, KV_W)),
                  _resident((A_KV, A_Q))],
        out_specs=[pl.BlockSpec((tm, 2 * B_W), lambda m: (m, 0)),
                   pl.BlockSpec((tm, 2 * A_Q), lambda m: (m, 0)),
                   tail(b_keep, B_W), tail(b_keep, B_W), tail(a_keep, A_KV), tail(a_keep, A_KV),
                   samp(B_W), samp(B_W), samp(A_KV), samp(A_KV)],
        out_shape=[jax.ShapeDtypeStruct((t, 2 * B_W), BF16),
                   jax.ShapeDtypeStruct((t, 2 * A_Q), BF16),
                   jax.ShapeDtypeStruct((batch, b_keep, B_W), F32),
                   jax.ShapeDtypeStruct((batch, b_keep, B_W), F32),
                   jax.ShapeDtypeStruct((batch, a_keep, A_KV), F32),
                   jax.ShapeDtypeStruct((batch, a_keep, A_KV), F32),
                   jax.ShapeDtypeStruct((ts, B_W), F32),
                   jax.ShapeDtypeStruct((ts, B_W), F32),
                   jax.ShapeDtypeStruct((ts, A_KV), F32),
                   jax.ShapeDtypeStruct((ts, A_KV), F32)],
        compiler_params=_params("arbitrary"),
        name="kv_proj",
    )(h, w_kv, sel)


def _matmul_kernel(h_ref, w_ref, o_ref):
    o_ref[...] = jnp.dot(h_ref[...], w_ref[...], preferred_element_type=F32).astype(o_ref.dtype)


def _rest_proj(h, w_rest, tm=1024, tn=2048):
    t = h.shape[0]
    return pl.pallas_call(
        _matmul_kernel,
        grid=(REST_W // tn, t // tm),
        in_specs=[pl.BlockSpec((tm, D_MODEL), lambda n, m: (m, 0)),
                  pl.BlockSpec((D_MODEL, tn), lambda n, m: (0, n))],
        out_specs=pl.BlockSpec((tm, tn), lambda n, m: (m, n)),
        out_shape=jax.ShapeDtypeStruct((t, REST_W), BF16),
        compiler_params=_params("arbitrary", "arbitrary"),
        name="rest_proj",
    )(h, w_rest)


def _mixer(q, k_ref, v_ref, start, band, bias_ref, sink_ref, first_valid, s_ref, e_ref):
    lane_head = lax.broadcasted_iota(jnp.int32, (CHUNK, GROUP_W), 1) // HEAD_DIM
    if first_valid is not None:
        valid = lax.broadcasted_iota(jnp.int32, (ROW_BLOCK, band), 1) >= first_valid

    def scores(g):
        cols = slice(g * GROUP_W, (g + 1) * GROUP_W)
        q4 = q[:, cols] * (HEAD_DIM ** -0.5)
        lhs = jnp.concatenate(
            [jnp.where(lane_head == r, q4, jnp.zeros_like(q4)) for r in range(GROUP)], axis=0)
        k4 = k_ref[pl.ds(start, band), cols]
        s_ref[g % 2] = lax.dot_general(lhs, k4, (((1,), (1,)), ((), ())),
                                       preferred_element_type=F32)

    def softmax(g):
        slot = g % 2
        denoms = []
        for rb in range(0, GROUP_ROWS, ROW_BLOCK):
            rows = slice(rb, rb + ROW_BLOCK)
            s = s_ref[slot, rows, :] + bias_ref[g, rows, :]
            if first_valid is not None:
                s = jnp.where(valid, s, NEG_INF)
            m = jnp.max(s, axis=-1, keepdims=True)
            if sink_ref is not None:
                sk = sink_ref[g, rows, :]
                m = jnp.maximum(m, sk)
            e = jnp.exp(s - m)
            l = jnp.sum(e, axis=-1, keepdims=True)
            if sink_ref is not None:
                l = l + jnp.exp(sk - m)
            e_ref[slot, rows, :] = e.astype(BF16)
            denoms.append(l)
        return jnp.concatenate(denoms, axis=0)

    def values(g, denom):
        cols = slice(g * GROUP_W, (g + 1) * GROUP_W)
        v4 = v_ref[pl.ds(start, band), cols]
        pv = jnp.dot(e_ref[g % 2], v4, preferred_element_type=F32) * (1.0 / denom)
        o4 = jnp.where(lane_head == 0, pv[:CHUNK], 0.0)
        for r in range(1, GROUP):
            o4 = o4 + jnp.where(lane_head == r, pv[r * CHUNK:(r + 1) * CHUNK], 0.0)
        return o4

    return scores, softmax, values


def _chunk_attention(mixer_a, mixer_b):
    outs = []
    for scores, softmax, values in (mixer_a, mixer_b):
        o = []
        for g in range(N_GROUPS):
            scores(g)
            o.append(values(g, softmax(g)))
        outs.append(jnp.concatenate(o, axis=1))
    return outs


def _score_scratch():
    return [pltpu.VMEM((2, GROUP_ROWS, A_BAND), F32), pltpu.VMEM((2, GROUP_ROWS, A_BAND), BF16),
            pltpu.VMEM((2, GROUP_ROWS, B_BAND), F32), pltpu.VMEM((2, GROUP_ROWS, B_BAND), BF16)]


def _attn_prompt_kernel(qa_ref, qb_ref, kb_ref, vb_ref, ka_ref, va_ref,
                        bias_a_ref, bias_b_ref, sink_ref, oa_ref, ob_ref,
                        kb_buf, vb_buf, ka_buf, va_buf, sa_ref, ea_ref, sb_ref, eb_ref):
    i = pl.program_id(1)
    for src, buf in ((kb_ref, kb_buf), (vb_ref, vb_buf), (ka_ref, ka_buf), (va_ref, va_buf)):
        @pl.when(i == 0)
        def _():
            buf[:Q_BLOCK] = jnp.zeros((Q_BLOCK, buf.shape[1]), BF16)

        @pl.when(i > 0)
        def _():
            buf[:Q_BLOCK] = buf[Q_BLOCK:]

        buf[Q_BLOCK:] = src[...]

    def chunk(c, carry, *, masked):
        r0 = pl.multiple_of(c * CHUNK, CHUNK)
        first_a = jnp.maximum(A_BACK_CHUNKS - c, 0) * CHUNK if masked else None
        first_b = (B_BACK_CHUNKS - c) * CHUNK if masked else None
        start_a = pl.multiple_of(r0 + (Q_BLOCK - A_BACK_CHUNKS * CHUNK), CHUNK)
        oa, ob = _chunk_attention(
            _mixer(qa_ref[pl.ds(r0, CHUNK), :], ka_buf, va_buf, start_a, A_BAND,
                   bias_a_ref, sink_ref, first_a, sa_ref, ea_ref),
            _mixer(qb_ref[pl.ds(r0, CHUNK), :], kb_buf, vb_buf, r0, B_BAND,
                   bias_b_ref, None, first_b, sb_ref, eb_ref))
        oa_ref[pl.ds(r0, CHUNK), :] = oa.astype(oa_ref.dtype)
        ob_ref[pl.ds(r0, CHUNK), :] = ob.astype(ob_ref.dtype)
        return carry

    n_chunks = Q_BLOCK // CHUNK

    @pl.when(i == 0)
    def _():
        lax.fori_loop(0, n_chunks, functools.partial(chunk, masked=True), 0)

    @pl.when(i > 0)
    def _():
        lax.fori_loop(0, n_chunks, functools.partial(chunk, masked=False), 0)


def _attn_prompt(zr, kvb, kva, bias_a, bias_b, sink, batch, seq):
    t = zr.shape[0]
    nblk = seq // Q_BLOCK
    blk = lambda col: pl.BlockSpec((Q_BLOCK, A_Q), lambda b, i: (b * nblk + i, col))
    return pl.pallas_call(
        _attn_prompt_kernel,
        grid=(batch, nblk),
        in_specs=[blk(0), blk(1), blk(0), blk(1), blk(0), blk(1),
                  _resident(bias_a.shape), _resident(bias_b.shape), _resident(sink.shape)],
        out_specs=[blk(0), blk(0)],
        out_shape=[jax.ShapeDtypeStruct((t, A_Q), BF16), jax.ShapeDtypeStruct((t, B_W), BF16)],
        scratch_shapes=[pltpu.VMEM((2 * Q_BLOCK, A_Q), BF16) for _ in range(4)] + _score_scratch(),
        compiler_params=_params("arbitrary", "arbitrary"),
        name="attn_prompt",
    )(zr, zr, kvb, kvb, kva, kva, bias_a, bias_b, sink)


def _attn_sample_kernel(qa_ref, qb_ref, kb_ref, vb_ref, ka_ref, va_ref,
                        ckb_ref, cvb_ref, cka_ref, cva_ref, sel_ref,
                        bias_a_ref, bias_b_ref, sink_ref, oa_in, ob_in, oa_ref, ob_ref,
                        kb_buf, vb_buf, ka_buf, va_buf, sa_ref, ea_ref, sb_ref, eb_ref):
    del oa_in, ob_in
    nb = ckb_ref.shape[0]
    kb_buf[:nb] = ckb_ref[...].astype(BF16)
    kb_buf[nb:] = kb_ref[...]
    vb_buf[:nb] = cvb_ref[...].astype(BF16)
    vb_buf[nb:] = vb_ref[...]
    na = cka_ref.shape[0]
    sel = sel_ref[...]
    ka_buf[:na] = jnp.dot(cka_ref[...].astype(BF16), sel, preferred_element_type=F32).astype(BF16)
    ka_buf[na:] = ka_ref[...]
    va_buf[:na] = jnp.dot(cva_ref[...].astype(BF16), sel, preferred_element_type=F32).astype(BF16)
    va_buf[na:] = va_ref[...]

    oa, ob = _chunk_attention(
        _mixer(qa_ref[...], ka_buf, va_buf, 0, A_BAND, bias_a_ref, sink_ref, None, sa_ref, ea_ref),
        _mixer(qb_ref[...], kb_buf, vb_buf, 0, B_BAND, bias_b_ref, None, None, sb_ref, eb_ref))
    oa_ref[...] = oa.astype(oa_ref.dtype)
    ob_ref[...] = ob.astype(ob_ref.dtype)


def _attn_sample(zr, kvb, kva, ckb, cvb, cka, cva, sel, bias_a, bias_b, sink, oa, ob, row0, layer):
    nb = ckb.shape[1]
    c0 = row0 // CHUNK
    qblk = lambda col: pl.BlockSpec((CHUNK, A_Q), lambda b: (c0 + b, col))
    cache = lambda a: pl.BlockSpec((None, None) + a.shape[2:], lambda b: (layer, b, 0, 0))
    any_spec = pl.BlockSpec(memory_space=pl.ANY)
    return pl.pallas_call(
        _attn_sample_kernel,
        grid=(nb,),
        in_specs=[qblk(0), qblk(1), qblk(0), qblk(1), qblk(0), qblk(1),
                  cache(ckb), cache(cvb), cache(cka), cache(cva), _resident(sel.shape),
                  _resident(bias_a.shape), _resident(bias_b.shape), _resident(sink.shape),
                  any_spec, any_spec],
        out_specs=[qblk(0), qblk(0)],
        out_shape=[jax.ShapeDtypeStruct(oa.shape, oa.dtype), jax.ShapeDtypeStruct(ob.shape, ob.dtype)],
        input_output_aliases={14: 0, 15: 1},
        scratch_shapes=[pltpu.VMEM((B_BAND, B_W), BF16), pltpu.VMEM((B_BAND, B_W), BF16),
                        pltpu.VMEM((A_BAND, A_Q), BF16), pltpu.VMEM((A_BAND, A_Q), BF16)]
                       + _score_scratch(),
        compiler_params=_params("arbitrary"),
        name="attn_sample",
    )(zr, zr, kvb, kvb, kva, kva, ckb, cvb, cka, cva, sel, bias_a, bias_b, sink, oa, ob)


def _merge_kernel(oa_ref, ob_ref, ga_ref, gb_ref, x_ref, wa_ref, wb_ref, wo_ref, o_ref):
    ta = jnp.dot(oa_ref[...], wa_ref[...], preferred_element_type=F32)
    tb = jnp.dot(ob_ref[...], wb_ref[...], preferred_element_type=F32)
    mixed = (jax.nn.sigmoid(ga_ref[...].astype(F32)) * ta
             + jax.nn.sigmoid(gb_ref[...].astype(F32)) * tb)
    o_ref[...] = x_ref[...] + jnp.dot(mixed.astype(BF16), wo_ref[...], preferred_element_type=F32)


def _merge(oa, ob, zr, x, wa, wb, wo, tm=256):
    t = x.shape[0]
    gate_blk = (A_Q + B_W) // D_MODEL
    return pl.pallas_call(
        _merge_kernel,
        grid=(t // tm,),
        in_specs=[pl.BlockSpec((tm, A_Q), lambda m: (m, 0)),
                  pl.BlockSpec((tm, B_W), lambda m: (m, 0)),
                  pl.BlockSpec((tm, D_MODEL), lambda m: (m, gate_blk)),
                  pl.BlockSpec((tm, D_MODEL), lambda m: (m, gate_blk + 1)),
                  pl.BlockSpec((tm, D_MODEL), lambda m: (m, 0)),
                  _resident(wa.shape), _resident(wb.shape), _resident(wo.shape)],
        out_specs=pl.BlockSpec((tm, D_MODEL), lambda m: (m, 0)),
        out_shape=jax.ShapeDtypeStruct((t, D_MODEL), F32),
        compiler_params=_params("arbitrary"),
        name="merge",
    )(oa, ob, zr, zr, x, wa, wb, wo)


def _ffn_kernel(x_ref, g_ref, wg_ref, wu_ref, wd_ref, o_ref, h_ref):
    f = pl.program_id(1)

    @pl.when(f == 0)
    def _():
        x = x_ref[...]
        h_ref[...] = _rms(x, g_ref[...]).astype(BF16)
        o_ref[...] = x

    h = h_ref[...]
    gate = jnp.dot(h, wg_ref[...], preferred_element_type=F32)
    up = jnp.dot(h, wu_ref[...], preferred_element_type=F32)
    act = (jax.nn.silu(gate) * up).astype(BF16)
    o_ref[...] += jnp.dot(act, wd_ref[...], preferred_element_type=F32)


def _ffn(x, g, wg, wu, wd, tm=1024, tf=512):
    t = x.shape[0]
    d_ff = wg.shape[1]
    return pl.pallas_call(
        _ffn_kernel,
        grid=(t // tm, d_ff // tf),
        in_specs=[pl.BlockSpec((tm, D_MODEL), lambda m, f: (m, 0), pipeline_mode=pl.Buffered(1)),
                  pl.BlockSpec((1, D_MODEL), lambda m, f: (0, 0)),
                  pl.BlockSpec((D_MODEL, tf), lambda m, f: (0, f)),
                  pl.BlockSpec((D_MODEL, tf), lambda m, f: (0, f)),
                  pl.BlockSpec((tf, D_MODEL), lambda m, f: (f, 0))],
        out_specs=pl.BlockSpec((tm, D_MODEL), lambda m, f: (m, 0)),
        out_shape=jax.ShapeDtypeStruct((t, D_MODEL), F32),
        scratch_shapes=[pltpu.VMEM((tm, D_MODEL), BF16)],
        compiler_params=_params("arbitrary", "arbitrary"),
        name="ffn",
    )(x, g.reshape(1, D_MODEL), wg, wu, wd)


def _ple_update(x_ref, pp_ref, ps_ref, wpg_ref, wp_ref, n_p):
    x = x_ref[...]
    p = jnp.where(pl.program_id(0) < n_p, pp_ref[...], ps_ref[...])
    gate = jax.nn.sigmoid(jnp.dot(x.astype(BF16), wpg_ref[...], preferred_element_type=F32))
    emb = jnp.dot(p.astype(BF16), wp_ref[...], preferred_element_type=F32)
    return x + gate * emb


def _ple_kernel(x_ref, pp_ref, ps_ref, wpg_ref, wp_ref, g_ref, o_ref, h_ref, *, n_p):
    y = _ple_update(x_ref, pp_ref, ps_ref, wpg_ref, wp_ref, n_p)
    o_ref[...] = y
    h_ref[...] = _rms(y, g_ref[...]).astype(h_ref.dtype)


def _ple_final_kernel(x_ref, pp_ref, ps_ref, wpg_ref, wp_ref, g_ref, yp_ref, ys_ref, *, n_p):
    y = _rms(_ple_update(x_ref, pp_ref, ps_ref, wpg_ref, wp_ref, n_p), g_ref[...])
    m = pl.program_id(0)

    @pl.when(m < n_p)
    def _():
        yp_ref[...] = y

    @pl.when(m >= n_p)
    def _():
        ys_ref[...] = y


def _ple(x, pp, ps, layer, wpg, wp, g_next, final, tm=512):
    t = x.shape[0]
    tp, ts = pp.shape[1], ps.shape[1]
    n_p = tp // tm
    row = lambda m: (m, 0)
    in_specs = [pl.BlockSpec((tm, D_MODEL), row),
                pl.BlockSpec((None, tm, PLE_DIM), lambda m: (layer, _prompt_tile(m, n_p), 0)),
                pl.BlockSpec((None, tm, PLE_DIM), lambda m: (layer, _sample_tile(m, n_p), 0)),
                _resident(wpg.shape), _resident(wp.shape),
                pl.BlockSpec((1, D_MODEL), lambda m: (0, 0))]
    if final:
        body = _ple_final_kernel
        out_specs = [pl.BlockSpec((tm, D_MODEL), lambda m: (_prompt_tile(m, n_p), 0)),
                     pl.BlockSpec((tm, D_MODEL), lambda m: (_sample_tile(m, n_p), 0))]
        out_shape = [jax.ShapeDtypeStruct((tp, D_MODEL), F32), jax.ShapeDtypeStruct((ts, D_MODEL), F32)]
    else:
        body = _ple_kernel
        out_specs = [pl.BlockSpec((tm, D_MODEL), row), pl.BlockSpec((tm, D_MODEL), row)]
        out_shape = [jax.ShapeDtypeStruct((t, D_MODEL), F32), jax.ShapeDtypeStruct((t, D_MODEL), BF16)]
    return pl.pallas_call(
        functools.partial(body, n_p=n_p),
        grid=(t // tm,),
        in_specs=in_specs,
        out_specs=out_specs,
        out_shape=out_shape,
        compiler_params=_params("arbitrary"),
        name="ple",
    )(x, pp, ps, wpg, wp, g_next.reshape(1, D_MODEL))


def kernel(x_prompt, x_sample, cache_a_k, cache_a_v, cache_b_k, cache_b_v, p_prompt, p_sample,
           attn_norm, w_in, a_sink, t5_table, b_rel_table, w_a_out, w_b_out, w_o, ffn_norm,
           w_gate, w_up, w_down, w_ple, w_ple_gate, final_norm):
    depth = w_in.shape[0]
    batch, seq, _ = x_prompt.shape
    dec_batch, dec_seq, _ = x_sample.shape
    tp = batch * seq
    ts = dec_batch * dec_seq
    assert dec_seq == CHUNK and seq % Q_BLOCK == 0
    assert cache_a_k.shape[2] == A_BACK_CHUNKS * CHUNK and cache_b_k.shape[2] == B_BACK_CHUNKS * CHUNK
    a_keep = min(A_BACK_CHUNKS * CHUNK, seq)
    b_keep = min(B_BACK_CHUNKS * CHUNK, seq)

    o_ka = A_Q
    o_va = o_ka + A_KV
    o_qb = o_va + A_KV
    o_kb = o_qb + B_W
    o_vb = o_kb + B_W
    o_ga = o_vb + B_W

    src = jnp.arange(A_KV, dtype=jnp.int32)
    dst = jnp.arange(A_Q, dtype=jnp.int32)
    sel = ((src[:, None] // HEAD_DIM == dst[None, :] // GROUP_W)
           & (src[:, None] % HEAD_DIM == dst[None, :] % HEAD_DIM)).astype(BF16)

    bias_a = _bias_lookup(_t5_bucket(_band_rel(A_BAND)), t5_table, A_BAND)
    idx_b = jnp.clip(-_band_rel(B_BAND), -B_REL_CLIP, B_REL_CLIP) + B_REL_CLIP

    pp = p_prompt.reshape(depth, tp, PLE_DIM)
    ps = p_sample.reshape(depth, ts, PLE_DIM)
    ckb = cache_b_k.reshape(depth, dec_batch, -1, B_W)
    cvb = cache_b_v.reshape(depth, dec_batch, -1, B_W)
    cka = cache_a_k.reshape(depth, dec_batch, -1, A_KV)
    cva = cache_a_v.reshape(depth, dec_batch, -1, A_KV)

    x, h = _norm0(x_prompt.reshape(tp, D_MODEL), x_sample.reshape(ts, D_MODEL), attn_norm[0])
    caches = ([], [], [], [], [], [], [], [])
    for i in range(depth):
        wi = w_in[i]
        w_kv = jnp.concatenate([wi[:, o_kb:o_vb], wi[:, o_vb:o_ga], wi[:, o_ka:o_va], wi[:, o_va:o_qb]],
                               axis=1).astype(BF16)
        w_rest = jnp.concatenate([wi[:, :A_Q], wi[:, o_qb:o_kb], wi[:, o_ga:]], axis=1).astype(BF16)

        kvb, kva, kbp, vbp, kap, vap, kbs, vbs, kas, vas = _kv_proj(
            h, w_kv, sel, batch, seq, ts, a_keep, b_keep)
        zr = _rest_proj(h, w_rest)

        bias_b = _bias_lookup(idx_b, b_rel_table[i], B_BAND)
        sink = jnp.repeat(a_sink[i], CHUNK).reshape(N_GROUPS, GROUP_ROWS, 1)

        oa, ob = _attn_prompt(zr, kvb, kva, bias_a, bias_b, sink, batch, seq)
        oa, ob = _attn_sample(zr, kvb, kva, ckb, cvb, cka, cva, sel, bias_a, bias_b, sink,
                              oa, ob, tp, i)

        x = _merge(oa, ob, zr, x, w_a_out[i].astype(BF16), w_b_out[i].astype(BF16), w_o[i].astype(BF16))
        x = _ffn(x, ffn_norm[i], w_gate[i].astype(BF16), w_up[i].astype(BF16), w_down[i].astype(BF16))
        last = i == depth - 1
        g_next = final_norm if last else attn_norm[i + 1]
        x, h = _ple(x, pp, ps, i, w_ple_gate[i].astype(BF16), w_ple[i].astype(BF16), g_next, last)

        pieces = (
            kap.reshape(batch, a_keep, A_KV_HEADS, HEAD_DIM),
            vap.reshape(batch, a_keep, A_KV_HEADS, HEAD_DIM),
            kbp.reshape(batch, b_keep, B_HEADS, HEAD_DIM),
            vbp.reshape(batch, b_keep, B_HEADS, HEAD_DIM),
            kas.reshape(dec_batch, dec_seq, A_KV_HEADS, HEAD_DIM),
            vas.reshape(dec_batch, dec_seq, A_KV_HEADS, HEAD_DIM),
            kbs.reshape(dec_batch, dec_seq, B_HEADS, HEAD_DIM),
            vbs.reshape(dec_batch, dec_seq, B_HEADS, HEAD_DIM),
        )
        for lst, piece in zip(caches, pieces):
            lst.append(piece)

    y_p, y_s = x, h
    return (y_p.reshape(batch, seq, D_MODEL), y_s.reshape(dec_batch, dec_seq, D_MODEL),
            *[jnp.stack(lst, 0) for lst in caches])
```

```python
import functools
import math

import jax
import jax.numpy as jnp
from jax import lax
from jax.experimental import pallas as pl
from jax.experimental.pallas import tpu as pltpu

D_MODEL = 2048
CHUNK = 64
HEAD_DIM = 64
A_HEADS = 16
A_KV_HEADS = 4
A_BACK_CHUNKS = 2
B_HEADS = 16
B_BACK_CHUNKS = 8
B_REL_CLIP = 128
T5_BUCKETS = 32
T5_MAX_DIST = 128
PLE_DIM = 256
EPS = 1e-6
NEG_INF = -1e30

A_Q = A_HEADS * HEAD_DIM
A_KV = A_KV_HEADS * HEAD_DIM
B_W = B_HEADS * HEAD_DIM
A_BAND = (A_BACK_CHUNKS + 1) * CHUNK
B_BAND = (B_BACK_CHUNKS + 1) * CHUNK
A_WIDTH = 256
GROUP = 4
GROUP_W = GROUP * HEAD_DIM
GROUP_ROWS = GROUP * CHUNK
N_GROUPS = A_HEADS // GROUP
ROW_BLOCK = 32
CHUNK_UNROLL = 4
KV_W = 2 * B_W + 2 * A_KV
REST_W = A_Q + B_W + 2 * D_MODEL
Q_BLOCK = B_BACK_CHUNKS * CHUNK

VMEM_LIMIT = 56 * 1024 * 1024

F32 = jnp.float32
BF16 = jnp.bfloat16


def _params(*sem):
    return pltpu.CompilerParams(dimension_semantics=sem, vmem_limit_bytes=VMEM_LIMIT)


def _resident(shape):
    zeros = (0,) * len(shape)
    return pl.BlockSpec(shape, lambda *_: zeros, pipeline_mode=pl.Buffered(1))


def _rms(x, g):
    y = x * lax.rsqrt(jnp.mean(x * x, axis=-1, keepdims=True) + EPS)
    return y * g


def _bias_kernel(idx_ref, tab_ref, out_ref):
    t = tab_ref[...]
    hi = t.astype(BF16)
    r1 = t - hi.astype(F32)
    mid = r1.astype(BF16)
    lo = (r1 - mid.astype(F32)).astype(BF16)
    k = t.shape[1]
    n = idx_ref.shape[1]
    onehot = (lax.broadcasted_iota(jnp.int32, (k, n), 0) == idx_ref[...]).astype(BF16)
    dot = functools.partial(jnp.dot, preferred_element_type=F32)
    out_ref[...] = dot(hi, onehot) + (dot(mid, onehot) + dot(lo, onehot))


def _bias_lookup(idx, table, band):
    entries, heads = table.shape
    k = -(-entries // 128) * 128
    tab_t = jnp.pad(table.T, ((0, 0), (0, k - entries)))
    n = CHUNK * band
    tn = n // 8
    out = pl.pallas_call(
        _bias_kernel,
        grid=(n // tn,),
        in_specs=[pl.BlockSpec((1, tn), lambda j: (0, j)),
                  pl.BlockSpec((heads, k), lambda j: (0, 0))],
        out_specs=pl.BlockSpec((heads, tn), lambda j: (0, j)),
        out_shape=jax.ShapeDtypeStruct((heads, n), F32),
        compiler_params=_params("arbitrary"),
        name="bias_lookup",
    )(idx.reshape(1, n), tab_t)
    return out.reshape(N_GROUPS, GROUP_ROWS, band)


def _t5_bucket(rel):
    half = T5_BUCKETS // 2
    max_exact = half // 2
    ret = jnp.where(rel > 0, half, 0)
    n = jnp.abs(rel)
    nf = jnp.maximum(n, 1).astype(jnp.float32)
    large = max_exact + (jnp.log(nf / max_exact) / math.log(T5_MAX_DIST / max_exact)
                         * (half - max_exact)).astype(jnp.int32)
    large = jnp.minimum(large, half - 1)
    return ret + jnp.where(n < max_exact, n, large)


def _band_rel(band):
    qpos = jnp.arange(CHUNK, dtype=jnp.int32)
    kpos = jnp.arange(band, dtype=jnp.int32) - (band - CHUNK)
    return kpos[None, :] - qpos[:, None]


def _prompt_tile(m, n_p):
    return jnp.minimum(m, n_p - 1)


def _sample_tile(m, n_p):
    return jnp.maximum(m - n_p, 0)


def _norm0_kernel(xp_ref, xs_ref, g_ref, x_ref, h_ref, *, n_p):
    def emit(x):
        x_ref[...] = x
        h_ref[...] = _rms(x, g_ref[...]).astype(h_ref.dtype)

    m = pl.program_id(0)
    pl.when(m < n_p)(lambda: emit(xp_ref[...]))
    pl.when(m >= n_p)(lambda: emit(xs_ref[...]))


def _norm0(xp, xs, g, tm=512):
    tp, ts = xp.shape[0], xs.shape[0]
    n_p = tp // tm
    t = tp + ts
    return pl.pallas_call(
        functools.partial(_norm0_kernel, n_p=n_p),
        grid=(t // tm,),
        in_specs=[pl.BlockSpec((tm, D_MODEL), lambda m: (_prompt_tile(m, n_p), 0)),
                  pl.BlockSpec((tm, D_MODEL), lambda m: (_sample_tile(m, n_p), 0)),
                  pl.BlockSpec((1, D_MODEL), lambda m: (0, 0))],
        out_specs=[pl.BlockSpec((tm, D_MODEL), lambda m: (m, 0)),
                   pl.BlockSpec((tm, D_MODEL), lambda m: (m, 0))],
        out_shape=[jax.ShapeDtypeStruct((t, D_MODEL), F32),
                   jax.ShapeDtypeStruct((t, D_MODEL), BF16)],
        compiler_params=_params("arbitrary"),
        name="rms_norm",
    )(xp, xs, g.reshape(1, D_MODEL))


def _kv_proj_kernel(h_ref, w_ref, sel_ref, kvb_ref, kva_ref,
                    kbp_ref, vbp_ref, kap_ref, vap_ref, kbs_ref, vbs_ref, kas_ref, vas_ref,
                    *, n_p, tiles_per_seq):
    m = pl.program_id(0)
    r = jnp.dot(h_ref[...], w_ref[...], preferred_element_type=F32)
    rb = r.astype(BF16)
    kvb_ref[...] = rb[:, :2 * B_W]
    o_ka = 2 * B_W
    o_va = o_ka + A_KV
    sel = sel_ref[...]
    kva_ref[:, :A_Q] = jnp.dot(rb[:, o_ka:o_va], sel, preferred_element_type=F32).astype(BF16)
    kva_ref[:, A_Q:] = jnp.dot(rb[:, o_va:], sel, preferred_element_type=F32).astype(BF16)

    tm = r.shape[0]
    b_keep = kbp_ref.shape[0]
    a_keep = kap_ref.shape[0]

    @pl.when((m < n_p) & (m % tiles_per_seq == tiles_per_seq - 1))
    def _():
        kbp_ref[...] = r[tm - b_keep:, :B_W]
        vbp_ref[...] = r[tm - b_keep:, B_W:2 * B_W]
        kap_ref[...] = r[tm - a_keep:, o_ka:o_va]
        vap_ref[...] = r[tm - a_keep:, o_va:]

    @pl.when(m >= n_p)
    def _():
        kbs_ref[...] = r[:, :B_W]
        vbs_ref[...] = r[:, B_W:2 * B_W]
        kas_ref[...] = r[:, o_ka:o_va]
        vas_ref[...] = r[:, o_va:]


def _kv_proj(h, w_kv, sel, batch, seq, ts, a_keep, b_keep, tm=512):
    t = h.shape[0]
    n_p = batch * seq // tm
    tps = seq // tm
    assert seq % tm == 0 and ts % tm == 0 and a_keep <= tm and b_keep <= tm
    tail = lambda rows, w: pl.BlockSpec(
        (None, rows, w), lambda m: (jnp.minimum(m // tps, batch - 1), 0, 0))
    samp = lambda w: pl.BlockSpec((tm, w), lambda m: (_sample_tile(m, n_p), 0))
    return pl.pallas_call(
        functools.partial(_kv_proj_kernel, n_p=n_p, tiles_per_seq=tps),
        grid=(t // tm,),
        in_specs=[pl.BlockSpec((tm, D_MODEL), lambda m: (m, 0)),
                  _resident((D_MODEL, KV_W)),
                  _resident((A_KV, A_Q))],
        out_specs=[pl.BlockSpec((tm, 2 * B_W), lambda m: (m, 0)),
                   pl.BlockSpec((tm, 2 * A_Q), lambda m: (m, 0)),
                   tail(b_keep, B_W), tail(b_keep, B_W), tail(a_keep, A_KV), tail(a_keep, A_KV),
                   samp(B_W), samp(B_W), samp(A_KV), samp(A_KV)],
        out_shape=[jax.ShapeDtypeStruct((t, 2 * B_W), BF16),
                   jax.ShapeDtypeStruct((t, 2 * A_Q), BF16),
                   jax.ShapeDtypeStruct((batch, b_keep, B_W), F32),
                   jax.ShapeDtypeStruct((batch, b_keep, B_W), F32),
                   jax.ShapeDtypeStruct((batch, a_keep, A_KV), F32),
                   jax.ShapeDtypeStruct((batch, a_keep, A_KV), F32),
                   jax.ShapeDtypeStruct((ts, B_W), F32),
                   jax.ShapeDtypeStruct((ts, B_W), F32),
                   jax.ShapeDtypeStruct((ts, A_KV), F32),
                   jax.ShapeDtypeStruct((ts, A_KV), F32)],
        compiler_params=_params("arbitrary"),
        name="kv_proj",
    )(h, w_kv, sel)


def _matmul_kernel(h_ref, w_ref, o_ref):
    o_ref[...] = jnp.dot(h_ref[...], w_ref[...], preferred_element_type=F32).astype(o_ref.dtype)


def _rest_proj(h, w_rest, tm=1024, tn=2048):
    t = h.shape[0]
    return pl.pallas_call(
        _matmul_kernel,
        grid=(REST_W // tn, t // tm),
        in_specs=[pl.BlockSpec((tm, D_MODEL), lambda n, m: (m, 0)),
                  pl.BlockSpec((D_MODEL, tn), lambda n, m: (0, n))],
        out_specs=pl.BlockSpec((tm, tn), lambda n, m: (m, n)),
        out_shape=jax.ShapeDtypeStruct((t, REST_W), BF16),
        compiler_params=_params("arbitrary", "arbitrary"),
        name="rest_proj",
    )(h, w_rest)


def _mixer(q, k_ref, v_ref, start, band, bias_ref, first_valid, s_ref, e_ref):
    width = bias_ref.shape[-1]
    lane_head = lax.broadcasted_iota(jnp.int32, (CHUNK, GROUP_W), 1) // HEAD_DIM
    if first_valid is not None:
        valid = lax.broadcasted_iota(jnp.int32, (ROW_BLOCK, width), 1) >= first_valid

    def scores(g):
        cols = slice(g * GROUP_W, (g + 1) * GROUP_W)
        q4 = q[:, cols] * (HEAD_DIM ** -0.5)
        lhs = jnp.concatenate(
            [jnp.where(lane_head == r, q4, jnp.zeros_like(q4)) for r in range(GROUP)], axis=0)
        k4 = k_ref[pl.ds(start, band), cols]
        s_ref[g % 2, :, :band] = lax.dot_general(lhs, k4, (((1,), (1,)), ((), ())),
                                                 preferred_element_type=F32)

    def softmax(g):
        slot = g % 2
        denoms = []
        for rb in range(0, GROUP_ROWS, ROW_BLOCK):
            rows = slice(rb, rb + ROW_BLOCK)
            s = s_ref[slot, rows, :] + bias_ref[g, rows, :]
            if first_valid is not None:
                s = jnp.where(valid, s, NEG_INF)
            m = jnp.max(s, axis=-1, keepdims=True)
            e = jnp.exp(s - m)
            l = jnp.sum(e, axis=-1, keepdims=True)
            e_ref[slot, rows, :] = e.astype(BF16)
            denoms.append(jnp.broadcast_to(l, (ROW_BLOCK, 128)))
        return jnp.concatenate(denoms, axis=0)

    def values(g, denom):
        cols = slice(g * GROUP_W, (g + 1) * GROUP_W)
        v4 = v_ref[pl.ds(start, band), cols]
        pv = jnp.dot(e_ref[g % 2, :, :band], v4, preferred_element_type=F32)
        o4 = pv[:CHUNK]
        wide = lambda d: jnp.concatenate([d, d], axis=1)
        d4 = wide(denom[:CHUNK])
        for r in range(1, GROUP):
            o4 = jnp.where(lane_head == r, pv[r * CHUNK:(r + 1) * CHUNK], o4)
            d4 = jnp.where(lane_head == r, wide(denom[r * CHUNK:(r + 1) * CHUNK]), d4)
        return o4 * (1.0 / d4)

    return scores, softmax, values


def _chunk_attention(mixer_a, mixer_b):
    outs = []
    for scores, softmax, values in (mixer_a, mixer_b):
        o = []
        for g in range(N_GROUPS):
            scores(g)
            o.append(values(g, softmax(g)))
        outs.append(jnp.concatenate(o, axis=1))
    return outs


def _score_scratch():
    return [pltpu.VMEM((2, GROUP_ROWS, A_WIDTH), F32), pltpu.VMEM((2, GROUP_ROWS, A_WIDTH), BF16),
            pltpu.VMEM((2, GROUP_ROWS, B_BAND), F32), pltpu.VMEM((2, GROUP_ROWS, B_BAND), BF16)]


def _attn_prompt_kernel(qa_ref, qb_ref, kb_ref, vb_ref, ka_ref, va_ref,
                        bias_a_ref, bias_b_ref, oa_ref, ob_ref,
                        kb_buf, vb_buf, ka_buf, va_buf, sa_ref, ea_ref, sb_ref, eb_ref):
    i = pl.program_id(1)
    sa_ref[:, :, A_BAND:] = jnp.zeros((2, GROUP_ROWS, A_WIDTH - A_BAND), F32)
    for src, buf in ((kb_ref, kb_buf), (vb_ref, vb_buf), (ka_ref, ka_buf), (va_ref, va_buf)):
        @pl.when(i == 0)
        def _():
            buf[:Q_BLOCK] = jnp.zeros((Q_BLOCK, buf.shape[1]), BF16)

        @pl.when(i > 0)
        def _():
            buf[:Q_BLOCK] = buf[Q_BLOCK:]

        buf[Q_BLOCK:] = src[...]

    def chunk(c, carry, *, masked):
        r0 = pl.multiple_of(c * CHUNK, CHUNK)
        first_a = jnp.maximum(A_BACK_CHUNKS - c, 0) * CHUNK if masked else None
        first_b = (B_BACK_CHUNKS - c) * CHUNK if masked else None
        start_a = pl.multiple_of(r0 + (Q_BLOCK - A_BACK_CHUNKS * CHUNK), CHUNK)
        oa, ob = _chunk_attention(
            _mixer(qa_ref[pl.ds(r0, CHUNK), :], ka_buf, va_buf, start_a, A_BAND,
                   bias_a_ref, first_a, sa_ref, ea_ref),
            _mixer(qb_ref[pl.ds(r0, CHUNK), :], kb_buf, vb_buf, r0, B_BAND,
                   bias_b_ref, first_b, sb_ref, eb_ref))
        oa_ref[pl.ds(r0, CHUNK), :] = oa.astype(oa_ref.dtype)
        ob_ref[pl.ds(r0, CHUNK), :] = ob.astype(ob_ref.dtype)
        return carry

    n_chunks = Q_BLOCK // CHUNK

    @pl.when(i == 0)
    def _():
        lax.fori_loop(0, n_chunks, functools.partial(chunk, masked=True), 0)

    @pl.when(i > 0)
    def _():
        lax.fori_loop(0, n_chunks, functools.partial(chunk, masked=False), 0, unroll=CHUNK_UNROLL)


def _attn_prompt(zr, kvb, kva, bias_a, bias_b, batch, seq):
    nblk = seq // Q_BLOCK
    blk = lambda col: pl.BlockSpec((Q_BLOCK, A_Q), lambda b, i: (b * nblk + i, col))
    return pl.pallas_call(
        _attn_prompt_kernel,
        grid=(batch, nblk),
        in_specs=[blk(0), blk(1), blk(0), blk(1), blk(0), blk(1),
                  _resident(bias_a.shape), _resident(bias_b.shape)],
        out_specs=[blk(0), blk(0)],
        out_shape=[jax.ShapeDtypeStruct((batch * seq, A_Q), BF16),
                   jax.ShapeDtypeStruct((batch * seq, B_W), BF16)],
        scratch_shapes=[pltpu.VMEM((2 * Q_BLOCK, A_Q), BF16) for _ in range(4)] + _score_scratch(),
        compiler_params=_params("arbitrary", "arbitrary"),
        name="attn_prompt",
    )(zr, zr, kvb, kvb, kva, kva, bias_a, bias_b)


def _attn_sample_kernel(qa_ref, qb_ref, kb_ref, vb_ref, ka_ref, va_ref,
                        ckb_ref, cvb_ref, cka_ref, cva_ref, sel_ref,
                        bias_a_ref, bias_b_ref, oa_ref, ob_ref,
                        kb_buf, vb_buf, ka_buf, va_buf, sa_ref, ea_ref, sb_ref, eb_ref):
    sa_ref[:, :, A_BAND:] = jnp.zeros((2, GROUP_ROWS, A_WIDTH - A_BAND), F32)
    nb = ckb_ref.shape[0]
    kb_buf[:nb] = ckb_ref[...].astype(BF16)
    kb_buf[nb:] = kb_ref[...]
    vb_buf[:nb] = cvb_ref[...].astype(BF16)
    vb_buf[nb:] = vb_ref[...]
    na = cka_ref.shape[0]
    sel = sel_ref[...]
    ka_buf[:na] = jnp.dot(cka_ref[...].astype(BF16), sel, preferred_element_type=F32).astype(BF16)
    ka_buf[na:] = ka_ref[...]
    va_buf[:na] = jnp.dot(cva_ref[...].astype(BF16), sel, preferred_element_type=F32).astype(BF16)
    va_buf[na:] = va_ref[...]

    oa, ob = _chunk_attention(
        _mixer(qa_ref[...], ka_buf, va_buf, 0, A_BAND, bias_a_ref, None, sa_ref, ea_ref),
        _mixer(qb_ref[...], kb_buf, vb_buf, 0, B_BAND, bias_b_ref, None, sb_ref, eb_ref))
    oa_ref[...] = oa.astype(oa_ref.dtype)
    ob_ref[...] = ob.astype(ob_ref.dtype)


def _attn_sample(zr, kvb, kva, ckb, cvb, cka, cva, sel, bias_a, bias_b, row0, layer):
    nb = ckb.shape[1]
    c0 = row0 // CHUNK
    qblk = lambda col: pl.BlockSpec((CHUNK, A_Q), lambda b: (c0 + b, col))
    cache = lambda a: pl.BlockSpec((None, None) + a.shape[2:], lambda b: (layer, b, 0, 0))
    return pl.pallas_call(
        _attn_sample_kernel,
        grid=(nb,),
        in_specs=[qblk(0), qblk(1), qblk(0), qblk(1), qblk(0), qblk(1),
                  cache(ckb), cache(cvb), cache(cka), cache(cva), _resident(sel.shape),
                  _resident(bias_a.shape), _resident(bias_b.shape)],
        out_specs=[pl.BlockSpec((CHUNK, A_Q), lambda b: (b, 0)), pl.BlockSpec((CHUNK, B_W), lambda b: (b, 0))],
        out_shape=[jax.ShapeDtypeStruct((nb * CHUNK, A_Q), BF16),
                   jax.ShapeDtypeStruct((nb * CHUNK, B_W), BF16)],
        scratch_shapes=[pltpu.VMEM((B_BAND, B_W), BF16), pltpu.VMEM((B_BAND, B_W), BF16),
                        pltpu.VMEM((A_BAND, A_Q), BF16), pltpu.VMEM((A_BAND, A_Q), BF16)]
                       + _score_scratch(),
        compiler_params=_params("arbitrary"),
        name="attn_sample",
    )(zr, zr, kvb, kvb, kva, kva, ckb, cvb, cka, cva, sel, bias_a, bias_b)


def _merge_kernel(oap_ref, obp_ref, oas_ref, obs_ref, ga_ref, gb_ref, x_ref,
                  wa_ref, wb_ref, wo_ref, o_ref, *, n_p):
    from_prompt = pl.program_id(0) < n_p
    oa = jnp.where(from_prompt, oap_ref[...], oas_ref[...])
    ob = jnp.where(from_prompt, obp_ref[...], obs_ref[...])
    ta = jnp.dot(oa, wa_ref[...], preferred_element_type=F32)
    tb = jnp.dot(ob, wb_ref[...], preferred_element_type=F32)
    mixed = (jax.nn.sigmoid(ga_ref[...].astype(F32)) * ta
             + jax.nn.sigmoid(gb_ref[...].astype(F32)) * tb)
    o_ref[...] = x_ref[...] + jnp.dot(mixed.astype(BF16), wo_ref[...], preferred_element_type=F32)


def _merge(oa_p, ob_p, oa_s, ob_s, zr, x, wa, wb, wo, tm=256):
    t = x.shape[0]
    n_p = oa_p.shape[0] // tm
    gate_blk = (A_Q + B_W) // D_MODEL
    prompt = pl.BlockSpec((tm, A_Q), lambda m: (_prompt_tile(m, n_p), 0))
    sample = pl.BlockSpec((tm, A_Q), lambda m: (_sample_tile(m, n_p), 0))
    return pl.pallas_call(
        functools.partial(_merge_kernel, n_p=n_p),
        grid=(t // tm,),
        in_specs=[prompt, prompt, sample, sample,
                  pl.BlockSpec((tm, D_MODEL), lambda m: (m, gate_blk)),
                  pl.BlockSpec((tm, D_MODEL), lambda m: (m, gate_blk + 1)),
                  pl.BlockSpec((tm, D_MODEL), lambda m: (m, 0)),
                  _resident(wa.shape), _resident(wb.shape), _resident(wo.shape)],
        out_specs=pl.BlockSpec((tm, D_MODEL), lambda m: (m, 0)),
        out_shape=jax.ShapeDtypeStruct((t, D_MODEL), F32),
        compiler_params=_params("arbitrary"),
        name="merge",
    )(oa_p, ob_p, oa_s, ob_s, zr, zr, x, wa, wb, wo)


def _ffn_kernel(x_ref, g_ref, wg_ref, wu_ref, wd_ref, o_ref, h_ref):
    f = pl.program_id(1)

    @pl.when(f == 0)
    def _():
        x = x_ref[...]
        h_ref[...] = _rms(x, g_ref[...]).astype(BF16)
        o_ref[...] = x

    h = h_ref[...]
    gate = jnp.dot(h, wg_ref[...], preferred_element_type=F32)
    up = jnp.dot(h, wu_ref[...], preferred_element_type=F32)
    act = (jax.nn.silu(gate) * up).astype(BF16)
    o_ref[...] += jnp.dot(act, wd_ref[...], preferred_element_type=F32)


def _ffn_tiles(w, tf=512):
    d, d_ff = w.shape
    return w.astype(BF16).reshape(d, d_ff // tf, tf).transpose(1, 0, 2)


def _ffn(x, g, wg, wu, wd, tm=1024):
    t = x.shape[0]
    n_f, _, tf = wg.shape
    return pl.pallas_call(
        _ffn_kernel,
        grid=(t // tm, n_f),
        in_specs=[pl.BlockSpec((tm, D_MODEL), lambda m, f: (m, 0)),
                  pl.BlockSpec((1, D_MODEL), lambda m, f: (0, 0)),
                  pl.BlockSpec((None, D_MODEL, tf), lambda m, f: (f, 0, 0)),
                  pl.BlockSpec((None, D_MODEL, tf), lambda m, f: (f, 0, 0)),
                  pl.BlockSpec((tf, D_MODEL), lambda m, f: (f, 0))],
        out_specs=pl.BlockSpec((tm, D_MODEL), lambda m, f: (m, 0)),
        out_shape=jax.ShapeDtypeStruct((t, D_MODEL), F32),
        scratch_shapes=[pltpu.VMEM((tm, D_MODEL), BF16)],
        compiler_params=_params("arbitrary", "arbitrary"),
        name="ffn",
    )(x, g.reshape(1, D_MODEL), wg, wu, wd)


def _ple_update(x_ref, pp_ref, ps_ref, wpg_ref, wp_ref, n_p):
    x = x_ref[...]
    p = jnp.where(pl.program_id(0) < n_p, pp_ref[...], ps_ref[...])
    gate = jax.nn.sigmoid(jnp.dot(x.astype(BF16), wpg_ref[...], preferred_element_type=F32))
    emb = jnp.dot(p.astype(BF16), wp_ref[...], preferred_element_type=F32)
    return x + gate * emb


def _ple_kernel(x_ref, pp_ref, ps_ref, wpg_ref, wp_ref, g_ref, o_ref, h_ref, *, n_p):
    y = _ple_update(x_ref, pp_ref, ps_ref, wpg_ref, wp_ref, n_p)
    o_ref[...] = y
    h_ref[...] = _rms(y, g_ref[...]).astype(h_ref.dtype)


def _ple_final_kernel(x_ref, pp_ref, ps_ref, wpg_ref, wp_ref, g_ref, yp_ref, ys_ref, *, n_p):
    y = _rms(_ple_update(x_ref, pp_ref, ps_ref, wpg_ref, wp_ref, n_p), g_ref[...])
    m = pl.program_id(0)

    @pl.when(m < n_p)
    def _():
        yp_ref[...] = y

    @pl.when(m >= n_p)
    def _():
        ys_ref[...] = y


def _ple(x, pp, ps, layer, wpg, wp, g_next, final, tm=512):
    t = x.shape[0]
    tp, ts = pp.shape[1], ps.shape[1]
    n_p = tp // tm
    row = lambda m: (m, 0)
    in_specs = [pl.BlockSpec((tm, D_MODEL), row),
                pl.BlockSpec((None, tm, PLE_DIM), lambda m: (layer, _prompt_tile(m, n_p), 0)),
                pl.BlockSpec((None, tm, PLE_DIM), lambda m: (layer, _sample_tile(m, n_p), 0)),
                _resident(wpg.shape), _resident(wp.shape),
                pl.BlockSpec((1, D_MODEL), lambda m: (0, 0))]
    if final:
        body = _ple_final_kernel
        out_specs = [pl.BlockSpec((tm, D_MODEL), lambda m: (_prompt_tile(m, n_p), 0)),
                     pl.BlockSpec((tm, D_MODEL), lambda m: (_sample_tile(m, n_p), 0))]
        out_shape = [jax.ShapeDtypeStruct((tp, D_MODEL), F32), jax.ShapeDtypeStruct((ts, D_MODEL), F32)]
    else:
        body = _ple_kernel
        out_specs = [pl.BlockSpec((tm, D_MODEL), row), pl.BlockSpec((tm, D_MODEL), row)]
        out_shape = [jax.ShapeDtypeStruct((t, D_MODEL), F32), jax.ShapeDtypeStruct((t, D_MODEL), BF16)]
    return pl.pallas_call(
        functools.partial(body, n_p=n_p),
        grid=(t // tm,),
        in_specs=in_specs,
        out_specs=out_specs,
        out_shape=out_shape,
        compiler_params=_params("arbitrary"),
        name="ple",
    )(x, pp, ps, wpg, wp, g_next.reshape(1, D_MODEL))


def kernel(x_prompt, x_sample, cache_a_k, cache_a_v, cache_b_k, cache_b_v, p_prompt, p_sample,
           attn_norm, w_in, a_sink, t5_table, b_rel_table, w_a_out, w_b_out, w_o, ffn_norm,
           w_gate, w_up, w_down, w_ple, w_ple_gate, final_norm):
    depth = w_in.shape[0]
    batch, seq, _ = x_prompt.shape
    dec_batch, dec_seq, _ = x_sample.shape
    tp = batch * seq
    ts = dec_batch * dec_seq
    assert dec_seq == CHUNK and seq % Q_BLOCK == 0
    assert cache_a_k.shape[2] == A_BACK_CHUNKS * CHUNK and cache_b_k.shape[2] == B_BACK_CHUNKS * CHUNK
    a_keep = min(A_BACK_CHUNKS * CHUNK, seq)
    b_keep = min(B_BACK_CHUNKS * CHUNK, seq)

    o_ka = A_Q
    o_va = o_ka + A_KV
    o_qb = o_va + A_KV
    o_kb = o_qb + B_W
    o_vb = o_kb + B_W
    o_ga = o_vb + B_W

    src = jnp.arange(A_KV, dtype=jnp.int32)
    dst = jnp.arange(A_Q, dtype=jnp.int32)
    sel = ((src[:, None] // HEAD_DIM == dst[None, :] // GROUP_W)
           & (src[:, None] % HEAD_DIM == dst[None, :] % HEAD_DIM)).astype(BF16)

    bias_a = _bias_lookup(_t5_bucket(_band_rel(A_BAND)), t5_table, A_BAND)
    idx_b = jnp.clip(-_band_rel(B_BAND), -B_REL_CLIP, B_REL_CLIP) + B_REL_CLIP

    pp = p_prompt.reshape(depth, tp, PLE_DIM)
    ps = p_sample.reshape(depth, ts, PLE_DIM)
    ckb = cache_b_k.reshape(depth, dec_batch, -1, B_W)
    cvb = cache_b_v.reshape(depth, dec_batch, -1, B_W)
    cka = cache_a_k.reshape(depth, dec_batch, -1, A_KV)
    cva = cache_a_v.reshape(depth, dec_batch, -1, A_KV)

    x, h = _norm0(x_prompt.reshape(tp, D_MODEL), x_sample.reshape(ts, D_MODEL), attn_norm[0])
    caches = ([], [], [], [], [], [], [], [])
    for i in range(depth):
        wi = w_in[i]
        w_kv = jnp.concatenate([wi[:, o_kb:o_vb], wi[:, o_vb:o_ga], wi[:, o_ka:o_va], wi[:, o_va:o_qb]],
                               axis=1).astype(BF16)
        w_rest = jnp.concatenate([wi[:, :A_Q], wi[:, o_qb:o_kb], wi[:, o_ga:]], axis=1).astype(BF16)

        kvb, kva, kbp, vbp, kap, vap, kbs, vbs, kas, vas = _kv_proj(
            h, w_kv, sel, batch, seq, ts, a_keep, b_keep)
        zr = _rest_proj(h, w_rest)

        bias_b = _bias_lookup(idx_b, b_rel_table[i], B_BAND)
        sink = jnp.repeat(a_sink[i], CHUNK).reshape(N_GROUPS, GROUP_ROWS, 1)
        pad = jnp.full((N_GROUPS, GROUP_ROWS, A_WIDTH - A_BAND - 1), NEG_INF, F32)
        bias_a_l = jnp.concatenate([bias_a, sink, pad], axis=-1)

        oa_p, ob_p = _attn_prompt(zr, kvb, kva, bias_a_l, bias_b, batch, seq)
        oa_s, ob_s = _attn_sample(zr, kvb, kva, ckb, cvb, cka, cva, sel, bias_a_l, bias_b, tp, i)

        x = _merge(oa_p, ob_p, oa_s, ob_s, zr, x,
                   w_a_out[i].astype(BF16), w_b_out[i].astype(BF16), w_o[i].astype(BF16))
        x = _ffn(x, ffn_norm[i], _ffn_tiles(w_gate[i]), _ffn_tiles(w_up[i]), w_down[i].astype(BF16))
        last = i == depth - 1
        g_next = final_norm if last else attn_norm[i + 1]
        x, h = _ple(x, pp, ps, i, w_ple_gate[i].astype(BF16), w_ple[i].astype(BF16), g_next, last)

        pieces = (
            kap.reshape(batch, a_keep, A_KV_HEADS, HEAD_DIM),
            vap.reshape(batch, a_keep, A_KV_HEADS, HEAD_DIM),
            kbp.reshape(batch, b_keep, B_HEADS, HEAD_DIM),
            vbp.reshape(batch, b_keep, B_HEADS, HEAD_DIM),
            kas.reshape(dec_batch, dec_seq, A_KV_HEADS, HEAD_DIM),
            vas.reshape(dec_batch, dec_seq, A_KV_HEADS, HEAD_DIM),
            kbs.reshape(dec_batch, dec_seq, B_HEADS, HEAD_DIM),
            vbs.reshape(dec_batch, dec_seq, B_HEADS, HEAD_DIM),
        )
        for lst, piece in zip(caches, pieces):
            lst.append(piece)

    y_p, y_s = x, h
    return (y_p.reshape(batch, seq, D_MODEL), y_s.reshape(dec_batch, dec_seq, D_MODEL),
            *[jnp.stack(lst, 0) for lst in caches])
```

```python
import functools
import math

import jax
import jax.numpy as jnp
from jax import lax
from jax.experimental import pallas as pl
from jax.experimental.pallas import tpu as pltpu

D_MODEL = 2048
CHUNK = 64
HEAD_DIM = 64
A_HEADS = 16
A_KV_HEADS = 4
A_BACK_CHUNKS = 2
B_HEADS = 16
B_BACK_CHUNKS = 8
B_REL_CLIP = 128
T5_BUCKETS = 32
T5_MAX_DIST = 128
PLE_DIM = 256
EPS = 1e-6
NEG_INF = -1e30

A_Q = A_HEADS * HEAD_DIM
A_KV = A_KV_HEADS * HEAD_DIM
B_W = B_HEADS * HEAD_DIM
A_BAND = (A_BACK_CHUNKS + 1) * CHUNK
B_BAND = (B_BACK_CHUNKS + 1) * CHUNK
A_WIDTH = 256
GROUP = 4
GROUP_W = GROUP * HEAD_DIM
GROUP_ROWS = GROUP * CHUNK
N_GROUPS = A_HEADS // GROUP
ROW_BLOCK = 32
CHUNK_UNROLL = 4
KV_W = 2 * B_W + 2 * A_KV
REST_W = A_Q + B_W + 2 * D_MODEL
Q_BLOCK = B_BACK_CHUNKS * CHUNK

VMEM_LIMIT = 56 * 1024 * 1024

F32 = jnp.float32
BF16 = jnp.bfloat16


def _params(*sem):
    return pltpu.CompilerParams(dimension_semantics=sem, vmem_limit_bytes=VMEM_LIMIT)


def _resident(shape):
    zeros = (0,) * len(shape)
    return pl.BlockSpec(shape, lambda *_: zeros, pipeline_mode=pl.Buffered(1))


def _layer_resident(stacked, layer):
    index = (layer,) + (0,) * (stacked.ndim - 1)
    return pl.BlockSpec((None,) + stacked.shape[1:], lambda *_: index, pipeline_mode=pl.Buffered(1))


def _rms(x, g):
    y = x * lax.rsqrt(jnp.mean(x * x, axis=-1, keepdims=True) + EPS)
    return y * g


def _bias_kernel(idx_ref, tab_ref, out_ref):
    t = tab_ref[...]
    hi = t.astype(BF16)
    r1 = t - hi.astype(F32)
    mid = r1.astype(BF16)
    lo = (r1 - mid.astype(F32)).astype(BF16)
    k = t.shape[1]
    n = idx_ref.shape[1]
    onehot = (lax.broadcasted_iota(jnp.int32, (k, n), 0) == idx_ref[...]).astype(BF16)
    dot = functools.partial(jnp.dot, preferred_element_type=F32)
    out_ref[...] = dot(hi, onehot) + (dot(mid, onehot) + dot(lo, onehot))


def _bias_lookup(idx, table, band):
    entries, heads = table.shape
    k = -(-entries // 128) * 128
    tab_t = jnp.pad(table.T, ((0, 0), (0, k - entries)))
    n = CHUNK * band
    tn = n // 8
    out = pl.pallas_call(
        _bias_kernel,
        grid=(n // tn,),
        in_specs=[pl.BlockSpec((1, tn), lambda j: (0, j)),
                  pl.BlockSpec((heads, k), lambda j: (0, 0))],
        out_specs=pl.BlockSpec((heads, tn), lambda j: (0, j)),
        out_shape=jax.ShapeDtypeStruct((heads, n), F32),
        compiler_params=_params("arbitrary"),
        name="bias_lookup",
    )(idx.reshape(1, n), tab_t)
    return out.reshape(N_GROUPS, GROUP_ROWS, band)


def _t5_bucket(rel):
    half = T5_BUCKETS // 2
    max_exact = half // 2
    ret = jnp.where(rel > 0, half, 0)
    n = jnp.abs(rel)
    nf = jnp.maximum(n, 1).astype(jnp.float32)
    large = max_exact + (jnp.log(nf / max_exact) / math.log(T5_MAX_DIST / max_exact)
                         * (half - max_exact)).astype(jnp.int32)
    large = jnp.minimum(large, half - 1)
    return ret + jnp.where(n < max_exact, n, large)


def _band_rel(band):
    qpos = jnp.arange(CHUNK, dtype=jnp.int32)
    kpos = jnp.arange(band, dtype=jnp.int32) - (band - CHUNK)
    return kpos[None, :] - qpos[:, None]


O_KA = A_Q
O_VA = O_KA + A_KV
O_QB = O_VA + A_KV
O_KB = O_QB + B_W
O_VB = O_KB + B_W
O_GA = O_VB + B_W
IN_WIDTH = O_GA + 2 * D_MODEL
W_IN_BLOCK = 512


def _col_blocks(lo, hi):
    return list(range(lo // W_IN_BLOCK, hi // W_IN_BLOCK))


KV_BLOCKS = _col_blocks(O_KB, O_GA) + _col_blocks(O_KA, O_QB)
REST_BLOCKS = _col_blocks(0, A_Q) + _col_blocks(O_QB, O_KB) + _col_blocks(O_GA, IN_WIDTH)


def _lookup(j, table):
    return sum(jnp.where(j == k, v, 0) for k, v in enumerate(table))


def _split_w_in_kernel(w_ref, kv_ref, rest_ref):
    j = pl.program_id(1)

    @pl.when(j < len(KV_BLOCKS))
    def _():
        kv_ref[...] = w_ref[...].astype(BF16)

    @pl.when(j >= len(KV_BLOCKS))
    def _():
        rest_ref[...] = w_ref[...].astype(BF16)


def _split_w_in(w_in):
    depth = w_in.shape[0]
    n_kv = len(KV_BLOCKS)
    order = KV_BLOCKS + REST_BLOCKS
    blk = (None, D_MODEL, W_IN_BLOCK)
    return pl.pallas_call(
        _split_w_in_kernel,
        grid=(depth, len(order)),
        in_specs=[pl.BlockSpec(blk, lambda l, j: (l, 0, _lookup(j, order)))],
        out_specs=[pl.BlockSpec(blk, lambda l, j: (l, 0, jnp.minimum(j, n_kv - 1))),
                   pl.BlockSpec(blk, lambda l, j: (l, 0, jnp.maximum(j - n_kv, 0)))],
        out_shape=[jax.ShapeDtypeStruct((depth, D_MODEL, KV_W), BF16),
                   jax.ShapeDtypeStruct((depth, D_MODEL, REST_W), BF16)],
        compiler_params=_params("arbitrary", "arbitrary"),
        name="split_w_in",
    )(w_in)


def _cast_kernel(w_ref, o_ref):
    o_ref[...] = w_ref[...].astype(o_ref.dtype)


def _ffn_tiles(w, tf=512):
    depth, d, d_ff = w.shape
    return pl.pallas_call(
        _cast_kernel,
        grid=(depth, d_ff // tf),
        in_specs=[pl.BlockSpec((None, d, tf), lambda l, f: (l, 0, f))],
        out_specs=pl.BlockSpec((None, None, d, tf), lambda l, f: (l, f, 0, 0)),
        out_shape=jax.ShapeDtypeStruct((depth, d_ff // tf, d, tf), BF16),
        compiler_params=_params("arbitrary", "arbitrary"),
        name="ffn_tiles",
    )(w)


def _prompt_tile(m, n_p):
    return jnp.minimum(m, n_p - 1)


def _sample_tile(m, n_p):
    return jnp.maximum(m - n_p, 0)


def _norm0_kernel(xp_ref, xs_ref, g_ref, x_ref, h_ref, *, n_p):
    def emit(x):
        x_ref[...] = x
        h_ref[...] = _rms(x, g_ref[...]).astype(h_ref.dtype)

    m = pl.program_id(0)
    pl.when(m < n_p)(lambda: emit(xp_ref[...]))
    pl.when(m >= n_p)(lambda: emit(xs_ref[...]))


def _norm0(xp, xs, g, tm=512):
    tp, ts = xp.shape[0], xs.shape[0]
    n_p = tp // tm
    t = tp + ts
    return pl.pallas_call(
        functools.partial(_norm0_kernel, n_p=n_p),
        grid=(t // tm,),
        in_specs=[pl.BlockSpec((tm, D_MODEL), lambda m: (_prompt_tile(m, n_p), 0)),
                  pl.BlockSpec((tm, D_MODEL), lambda m: (_sample_tile(m, n_p), 0)),
                  pl.BlockSpec((1, D_MODEL), lambda m: (0, 0))],
        out_specs=[pl.BlockSpec((tm, D_MODEL), lambda m: (m, 0)),
                   pl.BlockSpec((tm, D_MODEL), lambda m: (m, 0))],
        out_shape=[jax.ShapeDtypeStruct((t, D_MODEL), F32),
                   jax.ShapeDtypeStruct((t, D_MODEL), BF16)],
        compiler_params=_params("arbitrary"),
        name="rms_norm",
    )(xp, xs, g.reshape(1, D_MODEL))


def _kv_proj_kernel(h_ref, w_ref, sel_ref, kvb_ref, kva_ref,
                    kbp_ref, vbp_ref, kap_ref, vap_ref, kbs_ref, vbs_ref, kas_ref, vas_ref,
                    *, n_p, tiles_per_seq):
    m = pl.program_id(0)
    r = jnp.dot(h_ref[...], w_ref[...], preferred_element_type=F32)
    rb = r.astype(BF16)
    kvb_ref[...] = rb[:, :2 * B_W]
    o_ka = 2 * B_W
    o_va = o_ka + A_KV
    sel = sel_ref[...]
    kva_ref[:, :A_Q] = jnp.dot(rb[:, o_ka:o_va], sel, preferred_element_type=F32).astype(BF16)
    kva_ref[:, A_Q:] = jnp.dot(rb[:, o_va:], sel, preferred_element_type=F32).astype(BF16)

    tm = r.shape[0]
    b_keep = kbp_ref.shape[0]
    a_keep = kap_ref.shape[0]

    @pl.when((m < n_p) & (m % tiles_per_seq == tiles_per_seq - 1))
    def _():
        kbp_ref[...] = r[tm - b_keep:, :B_W]
        vbp_ref[...] = r[tm - b_keep:, B_W:2 * B_W]
        kap_ref[...] = r[tm - a_keep:, o_ka:o_va]
        vap_ref[...] = r[tm - a_keep:, o_va:]

    @pl.when(m >= n_p)
    def _():
        kbs_ref[...] = r[:, :B_W]
        vbs_ref[...] = r[:, B_W:2 * B_W]
        kas_ref[...] = r[:, o_ka:o_va]
        vas_ref[...] = r[:, o_va:]


def _kv_proj(h, w_kv, layer, sel, batch, seq, ts, a_keep, b_keep, tm=512):
    t = h.shape[0]
    n_p = batch * seq // tm
    tps = seq // tm
    assert seq % tm == 0 and ts % tm == 0 and a_keep <= tm and b_keep <= tm
    tail = lambda rows, w: pl.BlockSpec(
        (None, rows, w), lambda m: (jnp.minimum(m // tps, batch - 1), 0, 0))
    samp = lambda w: pl.BlockSpec((tm, w), lambda m: (_sample_tile(m, n_p), 0))
    return pl.pallas_call(
        functools.partial(_kv_proj_kernel, n_p=n_p, tiles_per_seq=tps),
        grid=(t // tm,),
        in_specs=[pl.BlockSpec((tm, D_MODEL), lambda m: (m, 0)),
                  _layer_resident(w_kv, layer),
                  _resident((A_KV, A_Q))],
        out_specs=[pl.BlockSpec((tm, 2 * B_W), lambda m: (m, 0)),
                   pl.BlockSpec((tm, 2 * A_Q), lambda m: (m, 0)),
                   tail(b_keep, B_W), tail(b_keep, B_W), tail(a_keep, A_KV), tail(a_keep, A_KV),
                   samp(B_W), samp(B_W), samp(A_KV), samp(A_KV)],
        out_shape=[jax.ShapeDtypeStruct((t, 2 * B_W), BF16),
                   jax.ShapeDtypeStruct((t, 2 * A_Q), BF16),
                   jax.ShapeDtypeStruct((batch, b_keep, B_W), F32),
                   jax.ShapeDtypeStruct((batch, b_keep, B_W), F32),
                   jax.ShapeDtypeStruct((batch, a_keep, A_KV), F32),
                   jax.ShapeDtypeStruct((batch, a_keep, A_KV), F32),
                   jax.ShapeDtypeStruct((ts, B_W), F32),
                   jax.ShapeDtypeStruct((ts, B_W), F32),
                   jax.ShapeDtypeStruct((ts, A_KV), F32),
                   jax.ShapeDtypeStruct((ts, A_KV), F32)],
        compiler_params=_params("arbitrary"),
        name="kv_proj",
    )(h, w_kv, sel)


def _matmul_kernel(h_ref, w_ref, o_ref):
    o_ref[...] = jnp.dot(h_ref[...], w_ref[...], preferred_element_type=F32).astype(o_ref.dtype)


def _rest_proj(h, w_rest, layer, tm=1024, tn=2048):
    t = h.shape[0]
    return pl.pallas_call(
        _matmul_kernel,
        grid=(REST_W // tn, t // tm),
        in_specs=[pl.BlockSpec((tm, D_MODEL), lambda n, m: (m, 0)),
                  pl.BlockSpec((None, D_MODEL, tn), lambda n, m: (layer, 0, n))],
        out_specs=pl.BlockSpec((tm, tn), lambda n, m: (m, n)),
        out_shape=jax.ShapeDtypeStruct((t, REST_W), BF16),
        compiler_params=_params("arbitrary", "arbitrary"),
        name="rest_proj",
    )(h, w_rest)


def _mixer(q, k_ref, v_ref, start, band, bias_ref, first_valid, s_ref, e_ref):
    width = bias_ref.shape[-1]
    lane_head = lax.broadcasted_iota(jnp.int32, (CHUNK, GROUP_W), 1) // HEAD_DIM
    if first_valid is not None:
        valid = lax.broadcasted_iota(jnp.int32, (ROW_BLOCK, width), 1) >= first_valid

    def scores(g):
        cols = slice(g * GROUP_W, (g + 1) * GROUP_W)
        q4 = q[:, cols] * (HEAD_DIM ** -0.5)
        lhs = jnp.concatenate(
            [jnp.where(lane_head == r, q4, jnp.zeros_like(q4)) for r in range(GROUP)], axis=0)
        k4 = k_ref[pl.ds(start, band), cols]
        s_ref[g % 2, :, :band] = lax.dot_general(lhs, k4, (((1,), (1,)), ((), ())),
                                                 preferred_element_type=F32)

    def softmax(g):
        slot = g % 2
        denoms = []
        for rb in range(0, GROUP_ROWS, ROW_BLOCK):
            rows = slice(rb, rb + ROW_BLOCK)
            s = s_ref[slot, rows, :] + bias_ref[g, rows, :]
            if first_valid is not None:
                s = jnp.where(valid, s, NEG_INF)
            m = jnp.max(s, axis=-1, keepdims=True)
            e = jnp.exp(s - m)
            l = jnp.sum(e, axis=-1, keepdims=True)
            e_ref[slot, rows, :] = e.astype(BF16)
            denoms.append(jnp.broadcast_to(l, (ROW_BLOCK, 128)))
        return jnp.concatenate(denoms, axis=0)

    def values(g, denom):
        cols = slice(g * GROUP_W, (g + 1) * GROUP_W)
        v4 = v_ref[pl.ds(start, band), cols]
        pv = jnp.dot(e_ref[g % 2, :, :band], v4, preferred_element_type=F32)
        o4 = pv[:CHUNK]
        wide = lambda d: jnp.concatenate([d, d], axis=1)
        d4 = wide(denom[:CHUNK])
        for r in range(1, GROUP):
            o4 = jnp.where(lane_head == r, pv[r * CHUNK:(r + 1) * CHUNK], o4)
            d4 = jnp.where(lane_head == r, wide(denom[r * CHUNK:(r + 1) * CHUNK]), d4)
        return o4 * (1.0 / d4)

    return scores, softmax, values


def _chunk_attention(mixer_a, mixer_b):
    outs = []
    for scores, softmax, values in (mixer_a, mixer_b):
        o = []
        for g in range(N_GROUPS):
            scores(g)
            o.append(values(g, softmax(g)))
        outs.append(jnp.concatenate(o, axis=1))
    return outs


def _score_scratch():
    return [pltpu.VMEM((2, GROUP_ROWS, A_WIDTH), F32), pltpu.VMEM((2, GROUP_ROWS, A_WIDTH), BF16),
            pltpu.VMEM((2, GROUP_ROWS, B_BAND), F32), pltpu.VMEM((2, GROUP_ROWS, B_BAND), BF16)]


def _attn_prompt_kernel(qa_ref, qb_ref, kb_ref, vb_ref, ka_ref, va_ref,
                        bias_a_ref, bias_b_ref, oa_ref, ob_ref,
                        kb_buf, vb_buf, ka_buf, va_buf, sa_ref, ea_ref, sb_ref, eb_ref):
    i = pl.program_id(1)
    sa_ref[:, :, A_BAND:] = jnp.zeros((2, GROUP_ROWS, A_WIDTH - A_BAND), F32)
    for src, buf in ((kb_ref, kb_buf), (vb_ref, vb_buf), (ka_ref, ka_buf), (va_ref, va_buf)):
        @pl.when(i == 0)
        def _():
            buf[:Q_BLOCK] = jnp.zeros((Q_BLOCK, buf.shape[1]), BF16)

        @pl.when(i > 0)
        def _():
            buf[:Q_BLOCK] = buf[Q_BLOCK:]

        buf[Q_BLOCK:] = src[...]

    def chunk(c, carry, *, masked):
        r0 = pl.multiple_of(c * CHUNK, CHUNK)
        first_a = jnp.maximum(A_BACK_CHUNKS - c, 0) * CHUNK if masked else None
        first_b = (B_BACK_CHUNKS - c) * CHUNK if masked else None
        start_a = pl.multiple_of(r0 + (Q_BLOCK - A_BACK_CHUNKS * CHUNK), CHUNK)
        oa, ob = _chunk_attention(
            _mixer(qa_ref[pl.ds(r0, CHUNK), :], ka_buf, va_buf, start_a, A_BAND,
                   bias_a_ref, first_a, sa_ref, ea_ref),
            _mixer(qb_ref[pl.ds(r0, CHUNK), :], kb_buf, vb_buf, r0, B_BAND,
                   bias_b_ref, first_b, sb_ref, eb_ref))
        oa_ref[pl.ds(r0, CHUNK), :] = oa.astype(oa_ref.dtype)
        ob_ref[pl.ds(r0, CHUNK), :] = ob.astype(ob_ref.dtype)
        return carry

    n_chunks = Q_BLOCK // CHUNK

    @pl.when(i == 0)
    def _():
        lax.fori_loop(0, n_chunks, functools.partial(chunk, masked=True), 0)

    @pl.when(i > 0)
    def _():
        lax.fori_loop(0, n_chunks, functools.partial(chunk, masked=False), 0, unroll=CHUNK_UNROLL)


def _attn_prompt(zr, kvb, kva, bias_a, bias_b, batch, seq):
    nblk = seq // Q_BLOCK
    blk = lambda col: pl.BlockSpec((Q_BLOCK, A_Q), lambda b, i: (b * nblk + i, col))
    return pl.pallas_call(
        _attn_prompt_kernel,
        grid=(batch, nblk),
        in_specs=[blk(0), blk(1), blk(0), blk(1), blk(0), blk(1),
                  _resident(bias_a.shape), _resident(bias_b.shape)],
        out_specs=[blk(0), blk(0)],
        out_shape=[jax.ShapeDtypeStruct((batch * seq, A_Q), BF16),
                   jax.ShapeDtypeStruct((batch * seq, B_W), BF16)],
        scratch_shapes=[pltpu.VMEM((2 * Q_BLOCK, A_Q), BF16) for _ in range(4)] + _score_scratch(),
        compiler_params=_params("arbitrary", "arbitrary"),
        name="attn_prompt",
    )(zr, zr, kvb, kvb, kva, kva, bias_a, bias_b)


def _attn_sample_kernel(qa_ref, qb_ref, kb_ref, vb_ref, ka_ref, va_ref,
                        ckb_ref, cvb_ref, cka_ref, cva_ref, sel_ref,
                        bias_a_ref, bias_b_ref, oa_ref, ob_ref,
                        kb_buf, vb_buf, ka_buf, va_buf, sa_ref, ea_ref, sb_ref, eb_ref):
    sa_ref[:, :, A_BAND:] = jnp.zeros((2, GROUP_ROWS, A_WIDTH - A_BAND), F32)
    nb = ckb_ref.shape[0]
    kb_buf[:nb] = ckb_ref[...].astype(BF16)
    kb_buf[nb:] = kb_ref[...]
    vb_buf[:nb] = cvb_ref[...].astype(BF16)
    vb_buf[nb:] = vb_ref[...]
    na = cka_ref.shape[0]
    sel = sel_ref[...]
    ka_buf[:na] = jnp.dot(cka_ref[...].astype(BF16), sel, preferred_element_type=F32).astype(BF16)
    ka_buf[na:] = ka_ref[...]
    va_buf[:na] = jnp.dot(cva_ref[...].astype(BF16), sel, preferred_element_type=F32).astype(BF16)
    va_buf[na:] = va_ref[...]

    oa, ob = _chunk_attention(
        _mixer(qa_ref[...], ka_buf, va_buf, 0, A_BAND, bias_a_ref, None, sa_ref, ea_ref),
        _mixer(qb_ref[...], kb_buf, vb_buf, 0, B_BAND, bias_b_ref, None, sb_ref, eb_ref))
    oa_ref[...] = oa.astype(oa_ref.dtype)
    ob_ref[...] = ob.astype(ob_ref.dtype)


def _attn_sample(zr, kvb, kva, ckb, cvb, cka, cva, sel, bias_a, bias_b, row0, layer):
    nb = ckb.shape[1]
    c0 = row0 // CHUNK
    qblk = lambda col: pl.BlockSpec((CHUNK, A_Q), lambda b: (c0 + b, col))
    cache = lambda a: pl.BlockSpec((None, None) + a.shape[2:], lambda b: (layer, b, 0, 0))
    return pl.pallas_call(
        _attn_sample_kernel,
        grid=(nb,),
        in_specs=[qblk(0), qblk(1), qblk(0), qblk(1), qblk(0), qblk(1),
                  cache(ckb), cache(cvb), cache(cka), cache(cva), _resident(sel.shape),
                  _resident(bias_a.shape), _resident(bias_b.shape)],
        out_specs=[pl.BlockSpec((CHUNK, A_Q), lambda b: (b, 0)), pl.BlockSpec((CHUNK, B_W), lambda b: (b, 0))],
        out_shape=[jax.ShapeDtypeStruct((nb * CHUNK, A_Q), BF16),
                   jax.ShapeDtypeStruct((nb * CHUNK, B_W), BF16)],
        scratch_shapes=[pltpu.VMEM((B_BAND, B_W), BF16), pltpu.VMEM((B_BAND, B_W), BF16),
                        pltpu.VMEM((A_BAND, A_Q), BF16), pltpu.VMEM((A_BAND, A_Q), BF16)]
                       + _score_scratch(),
        compiler_params=_params("arbitrary"),
        name="attn_sample",
    )(zr, zr, kvb, kvb, kva, kva, ckb, cvb, cka, cva, sel, bias_a, bias_b)


def _merge_kernel(oap_ref, obp_ref, oas_ref, obs_ref, ga_ref, gb_ref, x_ref,
                  wa_ref, wb_ref, wo_ref, o_ref, *, n_p):
    from_prompt = pl.program_id(0) < n_p
    oa = jnp.where(from_prompt, oap_ref[...], oas_ref[...])
    ob = jnp.where(from_prompt, obp_ref[...], obs_ref[...])
    ta = jnp.dot(oa, wa_ref[...], preferred_element_type=F32)
    tb = jnp.dot(ob, wb_ref[...], preferred_element_type=F32)
    mixed = (jax.nn.sigmoid(ga_ref[...].astype(F32)) * ta
             + jax.nn.sigmoid(gb_ref[...].astype(F32)) * tb)
    o_ref[...] = x_ref[...] + jnp.dot(mixed.astype(BF16), wo_ref[...], preferred_element_type=F32)


def _merge(oa_p, ob_p, oa_s, ob_s, zr, x, wa, wb, wo, layer, tm=256):
    t = x.shape[0]
    n_p = oa_p.shape[0] // tm
    gate_blk = (A_Q + B_W) // D_MODEL
    prompt = pl.BlockSpec((tm, A_Q), lambda m: (_prompt_tile(m, n_p), 0))
    sample = pl.BlockSpec((tm, A_Q), lambda m: (_sample_tile(m, n_p), 0))
    return pl.pallas_call(
        functools.partial(_merge_kernel, n_p=n_p),
        grid=(t // tm,),
        in_specs=[prompt, prompt, sample, sample,
                  pl.BlockSpec((tm, D_MODEL), lambda m: (m, gate_blk)),
                  pl.BlockSpec((tm, D_MODEL), lambda m: (m, gate_blk + 1)),
                  pl.BlockSpec((tm, D_MODEL), lambda m: (m, 0)),
                  _layer_resident(wa, layer), _layer_resident(wb, layer), _layer_resident(wo, layer)],
        out_specs=pl.BlockSpec((tm, D_MODEL), lambda m: (m, 0)),
        out_shape=jax.ShapeDtypeStruct((t, D_MODEL), F32),
        compiler_params=_params("arbitrary"),
        name="merge",
    )(oa_p, ob_p, oa_s, ob_s, zr, zr, x, wa, wb, wo)


def _ffn_kernel(x_ref, g_ref, wg_ref, wu_ref, wd_ref, o_ref, h_ref):
    f = pl.program_id(1)

    @pl.when(f == 0)
    def _():
        x = x_ref[...]
        h_ref[...] = _rms(x, g_ref[...]).astype(BF16)
        o_ref[...] = x

    h = h_ref[...]
    gate = jnp.dot(h, wg_ref[...], preferred_element_type=F32)
    up = jnp.dot(h, wu_ref[...], preferred_element_type=F32)
    act = (jax.nn.silu(gate) * up).astype(BF16)
    o_ref[...] += jnp.dot(act, wd_ref[...], preferred_element_type=F32)


def _ffn(x, g, wg, wu, wd, layer, tm=1024):
    t = x.shape[0]
    _, n_f, _, tf = wg.shape
    return pl.pallas_call(
        _ffn_kernel,
        grid=(t // tm, n_f),
        in_specs=[pl.BlockSpec((tm, D_MODEL), lambda m, f: (m, 0)),
                  pl.BlockSpec((1, D_MODEL), lambda m, f: (0, 0)),
                  pl.BlockSpec((None, None, D_MODEL, tf), lambda m, f: (layer, f, 0, 0)),
                  pl.BlockSpec((None, None, D_MODEL, tf), lambda m, f: (layer, f, 0, 0)),
                  pl.BlockSpec((None, tf, D_MODEL), lambda m, f: (layer, f, 0))],
        out_specs=pl.BlockSpec((tm, D_MODEL), lambda m, f: (m, 0)),
        out_shape=jax.ShapeDtypeStruct((t, D_MODEL), F32),
        scratch_shapes=[pltpu.VMEM((tm, D_MODEL), BF16)],
        compiler_params=_params("arbitrary", "arbitrary"),
        name="ffn",
    )(x, g.reshape(1, D_MODEL), wg, wu, wd)


def _ple_update(x_ref, pp_ref, ps_ref, wpg_ref, wp_ref, n_p):
    x = x_ref[...]
    p = jnp.where(pl.program_id(0) < n_p, pp_ref[...], ps_ref[...])
    gate = jax.nn.sigmoid(jnp.dot(x.astype(BF16), wpg_ref[...], preferred_element_type=F32))
    emb = jnp.dot(p.astype(BF16), wp_ref[...], preferred_element_type=F32)
    return x + gate * emb


def _ple_kernel(x_ref, pp_ref, ps_ref, wpg_ref, wp_ref, g_ref, o_ref, h_ref, *, n_p):
    y = _ple_update(x_ref, pp_ref, ps_ref, wpg_ref, wp_ref, n_p)
    o_ref[...] = y
    h_ref[...] = _rms(y, g_ref[...]).astype(h_ref.dtype)


def _ple_final_kernel(x_ref, pp_ref, ps_ref, wpg_ref, wp_ref, g_ref, yp_ref, ys_ref, *, n_p):
    y = _rms(_ple_update(x_ref, pp_ref, ps_ref, wpg_ref, wp_ref, n_p), g_ref[...])
    m = pl.program_id(0)

    @pl.when(m < n_p)
    def _():
        yp_ref[...] = y

    @pl.when(m >= n_p)
    def _():
        ys_ref[...] = y


def _ple(x, pp, ps, layer, wpg, wp, g_next, final, tm=512):
    t = x.shape[0]
    tp, ts = pp.shape[1], ps.shape[1]
    n_p = tp // tm
    row = lambda m: (m, 0)
    in_specs = [pl.BlockSpec((tm, D_MODEL), row),
                pl.BlockSpec((None, tm, PLE_DIM), lambda m: (layer, _prompt_tile(m, n_p), 0)),
                pl.BlockSpec((None, tm, PLE_DIM), lambda m: (layer, _sample_tile(m, n_p), 0)),
                _layer_resident(wpg, layer), _layer_resident(wp, layer),
                pl.BlockSpec((1, D_MODEL), lambda m: (0, 0))]
    if final:
        body = _ple_final_kernel
        out_specs = [pl.BlockSpec((tm, D_MODEL), lambda m: (_prompt_tile(m, n_p), 0)),
                     pl.BlockSpec((tm, D_MODEL), lambda m: (_sample_tile(m, n_p), 0))]
        out_shape = [jax.ShapeDtypeStruct((tp, D_MODEL), F32), jax.ShapeDtypeStruct((ts, D_MODEL), F32)]
    else:
        body = _ple_kernel
        out_specs = [pl.BlockSpec((tm, D_MODEL), row), pl.BlockSpec((tm, D_MODEL), row)]
        out_shape = [jax.ShapeDtypeStruct((t, D_MODEL), F32), jax.ShapeDtypeStruct((t, D_MODEL), BF16)]
    return pl.pallas_call(
        functools.partial(body, n_p=n_p),
        grid=(t // tm,),
        in_specs=in_specs,
        out_specs=out_specs,
        out_shape=out_shape,
        compiler_params=_params("arbitrary"),
        name="ple",
    )(x, pp, ps, wpg, wp, g_next.reshape(1, D_MODEL))


def kernel(x_prompt, x_sample, cache_a_k, cache_a_v, cache_b_k, cache_b_v, p_prompt, p_sample,
           attn_norm, w_in, a_sink, t5_table, b_rel_table, w_a_out, w_b_out, w_o, ffn_norm,
           w_gate, w_up, w_down, w_ple, w_ple_gate, final_norm):
    depth = w_in.shape[0]
    batch, seq, _ = x_prompt.shape
    dec_batch, dec_seq, _ = x_sample.shape
    tp = batch * seq
    ts = dec_batch * dec_seq
    assert dec_seq == CHUNK and seq % Q_BLOCK == 0
    assert cache_a_k.shape[2] == A_BACK_CHUNKS * CHUNK and cache_b_k.shape[2] == B_BACK_CHUNKS * CHUNK
    a_keep = min(A_BACK_CHUNKS * CHUNK, seq)
    b_keep = min(B_BACK_CHUNKS * CHUNK, seq)

    src = jnp.arange(A_KV, dtype=jnp.int32)
    dst = jnp.arange(A_Q, dtype=jnp.int32)
    sel = ((src[:, None] // HEAD_DIM == dst[None, :] // GROUP_W)
           & (src[:, None] % HEAD_DIM == dst[None, :] % HEAD_DIM)).astype(BF16)

    bias_a = _bias_lookup(_t5_bucket(_band_rel(A_BAND)), t5_table, A_BAND)
    idx_b = jnp.clip(-_band_rel(B_BAND), -B_REL_CLIP, B_REL_CLIP) + B_REL_CLIP

    pp = p_prompt.reshape(depth, tp, PLE_DIM)
    ps = p_sample.reshape(depth, ts, PLE_DIM)
    ckb = cache_b_k.reshape(depth, dec_batch, -1, B_W)
    cvb = cache_b_v.reshape(depth, dec_batch, -1, B_W)
    cka = cache_a_k.reshape(depth, dec_batch, -1, A_KV)
    cva = cache_a_v.reshape(depth, dec_batch, -1, A_KV)

    w_kv, w_rest = _split_w_in(w_in)
    wg, wu = _ffn_tiles(w_gate), _ffn_tiles(w_up)
    wd, wa, wb, wo = (w.astype(BF16) for w in (w_down, w_a_out, w_b_out, w_o))
    wpg, wp = w_ple_gate.astype(BF16), w_ple.astype(BF16)

    x, h = _norm0(x_prompt.reshape(tp, D_MODEL), x_sample.reshape(ts, D_MODEL), attn_norm[0])
    caches = ([], [], [], [], [], [], [], [])
    for i in range(depth):
        kvb, kva, kbp, vbp, kap, vap, kbs, vbs, kas, vas = _kv_proj(
            h, w_kv, i, sel, batch, seq, ts, a_keep, b_keep)
        zr = _rest_proj(h, w_rest, i)

        bias_b = _bias_lookup(idx_b, b_rel_table[i], B_BAND)
        sink = jnp.repeat(a_sink[i], CHUNK).reshape(N_GROUPS, GROUP_ROWS, 1)
        pad = jnp.full((N_GROUPS, GROUP_ROWS, A_WIDTH - A_BAND - 1), NEG_INF, F32)
        bias_a_l = jnp.concatenate([bias_a, sink, pad], axis=-1)

        oa_p, ob_p = _attn_prompt(zr, kvb, kva, bias_a_l, bias_b, batch, seq)
        oa_s, ob_s = _attn_sample(zr, kvb, kva, ckb, cvb, cka, cva, sel, bias_a_l, bias_b, tp, i)

        x = _merge(oa_p, ob_p, oa_s, ob_s, zr, x, wa, wb, wo, i)
        x = _ffn(x, ffn_norm[i], wg, wu, wd, i)
        last = i == depth - 1
        g_next = final_norm if last else attn_norm[i + 1]
        x, h = _ple(x, pp, ps, i, wpg, wp, g_next, last)

        pieces = (
            kap.reshape(batch, a_keep, A_KV_HEADS, HEAD_DIM),
            vap.reshape(batch, a_keep, A_KV_HEADS, HEAD_DIM),
            kbp.reshape(batch, b_keep, B_HEADS, HEAD_DIM),
            vbp.reshape(batch, b_keep, B_HEADS, HEAD_DIM),
            kas.reshape(dec_batch, dec_seq, A_KV_HEADS, HEAD_DIM),
            vas.reshape(dec_batch, dec_seq, A_KV_HEADS, HEAD_DIM),
            kbs.reshape(dec_batch, dec_seq, B_HEADS, HEAD_DIM),
            vbs.reshape(dec_batch, dec_seq, B_HEADS, HEAD_DIM),
        )
        for lst, piece in zip(caches, pieces):
            lst.append(piece)

    y_p, y_s = x, h
    return (y_p.reshape(batch, seq, D_MODEL), y_s.reshape(dec_batch, dec_seq, D_MODEL),
            *[jnp.stack(lst, 0) for lst in caches])
```

```python
import functools
import math

import jax
import jax.numpy as jnp
from jax import lax
from jax.experimental import pallas as pl
from jax.experimental.pallas import tpu as pltpu

D_MODEL = 2048
CHUNK = 64
HEAD_DIM = 64
A_HEADS = 16
A_KV_HEADS = 4
A_BACK_CHUNKS = 2
B_HEADS = 16
B_BACK_CHUNKS = 8
B_REL_CLIP = 128
T5_BUCKETS = 32
T5_MAX_DIST = 128
PLE_DIM = 256
EPS = 1e-6
NEG_INF = -1e30

A_Q = A_HEADS * HEAD_DIM
A_KV = A_KV_HEADS * HEAD_DIM
B_W = B_HEADS * HEAD_DIM
A_BAND = (A_BACK_CHUNKS + 1) * CHUNK
B_BAND = (B_BACK_CHUNKS + 1) * CHUNK
A_WIDTH = 256
GROUP = 4
GROUP_W = GROUP * HEAD_DIM
GROUP_ROWS = GROUP * CHUNK
N_GROUPS = A_HEADS // GROUP
ROW_BLOCK = 32
CHUNK_UNROLL = 4
KV_W = 2 * A_KV + 2 * B_W
REST_W = A_Q + B_W + 2 * D_MODEL
Q_BLOCK = B_BACK_CHUNKS * CHUNK

VMEM_LIMIT = 56 * 1024 * 1024

F32 = jnp.float32
BF16 = jnp.bfloat16


def _params(*sem):
    return pltpu.CompilerParams(dimension_semantics=sem, vmem_limit_bytes=VMEM_LIMIT)


def _resident(shape):
    zeros = (0,) * len(shape)
    return pl.BlockSpec(shape, lambda *_: zeros, pipeline_mode=pl.Buffered(1))


def _layer_resident(stacked, layer):
    index = (layer,) + (0,) * (stacked.ndim - 1)
    return pl.BlockSpec((None,) + stacked.shape[1:], lambda *_: index, pipeline_mode=pl.Buffered(1))


def _rms(x, g):
    y = x * lax.rsqrt(jnp.mean(x * x, axis=-1, keepdims=True) + EPS)
    return y * g


def _bias_kernel(idx_ref, tab_ref, out_ref):
    t = tab_ref[...]
    hi = t.astype(BF16)
    r1 = t - hi.astype(F32)
    mid = r1.astype(BF16)
    lo = (r1 - mid.astype(F32)).astype(BF16)
    k = t.shape[1]
    n = idx_ref.shape[1]
    onehot = (lax.broadcasted_iota(jnp.int32, (k, n), 0) == idx_ref[...]).astype(BF16)
    dot = functools.partial(jnp.dot, preferred_element_type=F32)
    out_ref[...] = dot(hi, onehot) + (dot(mid, onehot) + dot(lo, onehot))


def _bias_lookup(idx, table, band):
    entries, heads = table.shape
    k = -(-entries // 128) * 128
    tab_t = jnp.pad(table.T, ((0, 0), (0, k - entries)))
    n = CHUNK * band
    tn = n // 8
    out = pl.pallas_call(
        _bias_kernel,
        grid=(n // tn,),
        in_specs=[pl.BlockSpec((1, tn), lambda j: (0, j)),
                  pl.BlockSpec((heads, k), lambda j: (0, 0))],
        out_specs=pl.BlockSpec((heads, tn), lambda j: (0, j)),
        out_shape=jax.ShapeDtypeStruct((heads, n), F32),
        compiler_params=_params("arbitrary"),
        name="bias_lookup",
    )(idx.reshape(1, n), tab_t)
    return out.reshape(N_GROUPS, GROUP_ROWS, band)


def _t5_bucket(rel):
    half = T5_BUCKETS // 2
    max_exact = half // 2
    ret = jnp.where(rel > 0, half, 0)
    n = jnp.abs(rel)
    nf = jnp.maximum(n, 1).astype(jnp.float32)
    large = max_exact + (jnp.log(nf / max_exact) / math.log(T5_MAX_DIST / max_exact)
                         * (half - max_exact)).astype(jnp.int32)
    large = jnp.minimum(large, half - 1)
    return ret + jnp.where(n < max_exact, n, large)


def _band_rel(band):
    qpos = jnp.arange(CHUNK, dtype=jnp.int32)
    kpos = jnp.arange(band, dtype=jnp.int32) - (band - CHUNK)
    return kpos[None, :] - qpos[:, None]


O_KA = A_Q
O_VA = O_KA + A_KV
O_QB = O_VA + A_KV
O_KB = O_QB + B_W
O_VB = O_KB + B_W
O_GA = O_VB + B_W
IN_WIDTH = O_GA + 2 * D_MODEL
W_IN_BLOCK = 512


def _col_blocks(lo, hi):
    return list(range(lo // W_IN_BLOCK, hi // W_IN_BLOCK))


KV_BLOCKS = _col_blocks(O_KA, O_QB) + _col_blocks(O_KB, O_GA)
REST_BLOCKS = _col_blocks(0, A_Q) + _col_blocks(O_QB, O_KB) + _col_blocks(O_GA, IN_WIDTH)


def _lookup(j, table):
    return sum(jnp.where(j == k, v, 0) for k, v in enumerate(table))


def _split_w_in_kernel(w_ref, kv_ref, rest_ref):
    j = pl.program_id(1)

    @pl.when(j < len(KV_BLOCKS))
    def _():
        kv_ref[...] = w_ref[...].astype(BF16)

    @pl.when(j >= len(KV_BLOCKS))
    def _():
        rest_ref[...] = w_ref[...].astype(BF16)


def _split_w_in(w_in):
    depth = w_in.shape[0]
    n_kv = len(KV_BLOCKS)
    order = KV_BLOCKS + REST_BLOCKS
    blk = (None, D_MODEL, W_IN_BLOCK)
    return pl.pallas_call(
        _split_w_in_kernel,
        grid=(depth, len(order)),
        in_specs=[pl.BlockSpec(blk, lambda l, j: (l, 0, _lookup(j, order)))],
        out_specs=[pl.BlockSpec(blk, lambda l, j: (l, 0, jnp.minimum(j, n_kv - 1))),
                   pl.BlockSpec(blk, lambda l, j: (l, 0, jnp.maximum(j - n_kv, 0)))],
        out_shape=[jax.ShapeDtypeStruct((depth, D_MODEL, KV_W), BF16),
                   jax.ShapeDtypeStruct((depth, D_MODEL, REST_W), BF16)],
        compiler_params=_params("arbitrary", "arbitrary"),
        name="split_w_in",
    )(w_in)


def _cast_kernel(w_ref, o_ref):
    o_ref[...] = w_ref[...].astype(o_ref.dtype)


def _ffn_tiles(w, tf=512):
    depth, d, d_ff = w.shape
    return pl.pallas_call(
        _cast_kernel,
        grid=(depth, d_ff // tf),
        in_specs=[pl.BlockSpec((None, d, tf), lambda l, f: (l, 0, f))],
        out_specs=pl.BlockSpec((None, None, d, tf), lambda l, f: (l, f, 0, 0)),
        out_shape=jax.ShapeDtypeStruct((depth, d_ff // tf, d, tf), BF16),
        compiler_params=_params("arbitrary", "arbitrary"),
        name="ffn_tiles",
    )(w)


def _prompt_tile(m, n_p):
    return jnp.minimum(m, n_p - 1)


def _sample_tile(m, n_p):
    return jnp.maximum(m - n_p, 0)


def _norm0_kernel(xp_ref, xs_ref, g_ref, x_ref, h_ref, *, n_p):
    def emit(x):
        x_ref[...] = x
        h_ref[...] = _rms(x, g_ref[...]).astype(h_ref.dtype)

    m = pl.program_id(0)
    pl.when(m < n_p)(lambda: emit(xp_ref[...]))
    pl.when(m >= n_p)(lambda: emit(xs_ref[...]))


def _norm0(xp, xs, g, tm=512):
    tp, ts = xp.shape[0], xs.shape[0]
    n_p = tp // tm
    t = tp + ts
    return pl.pallas_call(
        functools.partial(_norm0_kernel, n_p=n_p),
        grid=(t // tm,),
        in_specs=[pl.BlockSpec((tm, D_MODEL), lambda m: (_prompt_tile(m, n_p), 0)),
                  pl.BlockSpec((tm, D_MODEL), lambda m: (_sample_tile(m, n_p), 0)),
                  pl.BlockSpec((1, D_MODEL), lambda m: (0, 0))],
        out_specs=[pl.BlockSpec((tm, D_MODEL), lambda m: (m, 0)),
                   pl.BlockSpec((tm, D_MODEL), lambda m: (m, 0))],
        out_shape=[jax.ShapeDtypeStruct((t, D_MODEL), F32),
                   jax.ShapeDtypeStruct((t, D_MODEL), BF16)],
        compiler_params=_params("arbitrary"),
        name="rms_norm",
    )(xp, xs, g.reshape(1, D_MODEL))


def _kv_proj_kernel(h_ref, w_ref, sel_ref, kvb_ref, kva_ref,
                    kbp_ref, vbp_ref, kap_ref, vap_ref, kbs_ref, vbs_ref, kas_ref, vas_ref,
                    *, n_p, tiles_per_seq):
    m = pl.program_id(0)
    h = h_ref[...]
    ra = jnp.dot(h, w_ref[:, :2 * A_KV], preferred_element_type=F32)
    rab = ra.astype(BF16)
    sel = sel_ref[...]
    kva_ref[:, :A_Q] = jnp.dot(rab[:, :A_KV], sel, preferred_element_type=F32).astype(BF16)
    kva_ref[:, A_Q:] = jnp.dot(rab[:, A_KV:], sel, preferred_element_type=F32).astype(BF16)
    rb = jnp.dot(h, w_ref[:, 2 * A_KV:], preferred_element_type=F32)
    kvb_ref[...] = rb.astype(BF16)

    tm = h.shape[0]
    b_keep = kbp_ref.shape[0]
    a_keep = kap_ref.shape[0]

    @pl.when((m < n_p) & (m % tiles_per_seq == tiles_per_seq - 1))
    def _():
        kbp_ref[...] = rb[tm - b_keep:, :B_W]
        vbp_ref[...] = rb[tm - b_keep:, B_W:]
        kap_ref[...] = ra[tm - a_keep:, :A_KV]
        vap_ref[...] = ra[tm - a_keep:, A_KV:]

    @pl.when(m >= n_p)
    def _():
        kbs_ref[...] = rb[:, :B_W]
        vbs_ref[...] = rb[:, B_W:]
        kas_ref[...] = ra[:, :A_KV]
        vas_ref[...] = ra[:, A_KV:]


def _kv_proj(h, w_kv, layer, sel, batch, seq, ts, a_keep, b_keep, tm=512):
    t = h.shape[0]
    n_p = batch * seq // tm
    tps = seq // tm
    assert seq % tm == 0 and ts % tm == 0 and a_keep <= tm and b_keep <= tm
    tail = lambda rows, w: pl.BlockSpec(
        (None, rows, w), lambda m: (jnp.minimum(m // tps, batch - 1), 0, 0))
    samp = lambda w: pl.BlockSpec((tm, w), lambda m: (_sample_tile(m, n_p), 0))
    return pl.pallas_call(
        functools.partial(_kv_proj_kernel, n_p=n_p, tiles_per_seq=tps),
        grid=(t // tm,),
        in_specs=[pl.BlockSpec((tm, D_MODEL), lambda m: (m, 0)),
                  _layer_resident(w_kv, layer),
                  _resident((A_KV, A_Q))],
        out_specs=[pl.BlockSpec((tm, 2 * B_W), lambda m: (m, 0)),
                   pl.BlockSpec((tm, 2 * A_Q), lambda m: (m, 0)),
                   tail(b_keep, B_W), tail(b_keep, B_W), tail(a_keep, A_KV), tail(a_keep, A_KV),
                   samp(B_W), samp(B_W), samp(A_KV), samp(A_KV)],
        out_shape=[jax.ShapeDtypeStruct((t, 2 * B_W), BF16),
                   jax.ShapeDtypeStruct((t, 2 * A_Q), BF16),
                   jax.ShapeDtypeStruct((batch, b_keep, B_W), F32),
                   jax.ShapeDtypeStruct((batch, b_keep, B_W), F32),
                   jax.ShapeDtypeStruct((batch, a_keep, A_KV), F32),
                   jax.ShapeDtypeStruct((batch, a_keep, A_KV), F32),
                   jax.ShapeDtypeStruct((ts, B_W), F32),
                   jax.ShapeDtypeStruct((ts, B_W), F32),
                   jax.ShapeDtypeStruct((ts, A_KV), F32),
                   jax.ShapeDtypeStruct((ts, A_KV), F32)],
        compiler_params=_params("arbitrary"),
        name="kv_proj",
    )(h, w_kv, sel)


def _matmul_kernel(h_ref, w_ref, o_ref):
    o_ref[...] = jnp.dot(h_ref[...], w_ref[...], preferred_element_type=F32).astype(o_ref.dtype)


def _rest_proj(h, w_rest, layer, tm=1024, tn=2048):
    t = h.shape[0]
    return pl.pallas_call(
        _matmul_kernel,
        grid=(REST_W // tn, t // tm),
        in_specs=[pl.BlockSpec((tm, D_MODEL), lambda n, m: (m, 0)),
                  pl.BlockSpec((None, D_MODEL, tn), lambda n, m: (layer, 0, n))],
        out_specs=pl.BlockSpec((tm, tn), lambda n, m: (m, n)),
        out_shape=jax.ShapeDtypeStruct((t, REST_W), BF16),
        compiler_params=_params("arbitrary", "arbitrary"),
        name="rest_proj",
    )(h, w_rest)


def _mixer(q, k_ref, v_ref, start, band, bias_ref, first_valid, s_ref, e_ref):
    width = bias_ref.shape[-1]
    lane_head = lax.broadcasted_iota(jnp.int32, (CHUNK, GROUP_W), 1) // HEAD_DIM
    if first_valid is not None:
        valid = lax.broadcasted_iota(jnp.int32, (GROUP_ROWS, band), 1) >= first_valid

    def scores(g):
        cols = slice(g * GROUP_W, (g + 1) * GROUP_W)
        q4 = q[:, cols] * (HEAD_DIM ** -0.5)
        lhs = jnp.concatenate(
            [jnp.where(lane_head == r, q4, jnp.zeros_like(q4)) for r in range(GROUP)], axis=0)
        k4 = k_ref[pl.ds(start, band), cols]
        s = lax.dot_general(lhs, k4, (((1,), (1,)), ((), ())), preferred_element_type=F32)
        s = s + bias_ref[g, :, :band]
        if first_valid is not None:
            s = jnp.where(valid, s, NEG_INF)
        s_ref[g % 2, :, :band] = s
        if width > band:
            s_ref[g % 2, :, band:] = bias_ref[g, :, band:]

    def softmax(g):
        slot = g % 2
        maxes = [jnp.max(s_ref[slot, rb:rb + ROW_BLOCK, :], axis=-1, keepdims=True)
                 for rb in range(0, GROUP_ROWS, ROW_BLOCK)]
        denoms = []
        for i, rb in enumerate(range(0, GROUP_ROWS, ROW_BLOCK)):
            rows = slice(rb, rb + ROW_BLOCK)
            e = jnp.exp(s_ref[slot, rows, :] - maxes[i])
            l = jnp.sum(e, axis=-1, keepdims=True)
            e_ref[slot, rows, :] = e.astype(BF16)
            denoms.append(jnp.broadcast_to(l, (ROW_BLOCK, 128)))
        return jnp.concatenate(denoms, axis=0)

    def values(g, denom):
        cols = slice(g * GROUP_W, (g + 1) * GROUP_W)
        v4 = v_ref[pl.ds(start, band), cols]
        pv = jnp.dot(e_ref[g % 2, :, :band], v4, preferred_element_type=F32)
        o4 = pv[:CHUNK]
        wide = lambda d: jnp.concatenate([d, d], axis=1)
        d4 = wide(denom[:CHUNK])
        for r in range(1, GROUP):
            o4 = jnp.where(lane_head == r, pv[r * CHUNK:(r + 1) * CHUNK], o4)
            d4 = jnp.where(lane_head == r, wide(denom[r * CHUNK:(r + 1) * CHUNK]), d4)
        return o4 * (1.0 / d4)

    return scores, softmax, values


def _chunk_attention(mixer_a, mixer_b):
    outs = []
    for scores, softmax, values in (mixer_a, mixer_b):
        o = []
        for g in range(N_GROUPS):
            scores(g)
            o.append(values(g, softmax(g)))
        outs.append(jnp.concatenate(o, axis=1))
    return outs


def _score_scratch():
    return [pltpu.VMEM((2, GROUP_ROWS, A_WIDTH), F32), pltpu.VMEM((2, GROUP_ROWS, A_WIDTH), BF16),
            pltpu.VMEM((2, GROUP_ROWS, B_BAND), F32), pltpu.VMEM((2, GROUP_ROWS, B_BAND), BF16)]


def _attn_prompt_kernel(qa_ref, qb_ref, kb_ref, vb_ref, ka_ref, va_ref,
                        bias_a_ref, bias_b_ref, oa_ref, ob_ref,
                        kb_buf, vb_buf, ka_buf, va_buf, sa_ref, ea_ref, sb_ref, eb_ref):
    i = pl.program_id(1)
    for src, buf in ((kb_ref, kb_buf), (vb_ref, vb_buf), (ka_ref, ka_buf), (va_ref, va_buf)):
        @pl.when(i == 0)
        def _():
            buf[:Q_BLOCK] = jnp.zeros((Q_BLOCK, buf.shape[1]), BF16)

        @pl.when(i > 0)
        def _():
            buf[:Q_BLOCK] = buf[Q_BLOCK:]

        buf[Q_BLOCK:] = src[...]

    def chunk(c, carry, *, masked):
        r0 = pl.multiple_of(c * CHUNK, CHUNK)
        first_a = jnp.maximum(A_BACK_CHUNKS - c, 0) * CHUNK if masked else None
        first_b = (B_BACK_CHUNKS - c) * CHUNK if masked else None
        start_a = pl.multiple_of(r0 + (Q_BLOCK - A_BACK_CHUNKS * CHUNK), CHUNK)
        oa, ob = _chunk_attention(
            _mixer(qa_ref[pl.ds(r0, CHUNK), :], ka_buf, va_buf, start_a, A_BAND,
                   bias_a_ref, first_a, sa_ref, ea_ref),
            _mixer(qb_ref[pl.ds(r0, CHUNK), :], kb_buf, vb_buf, r0, B_BAND,
                   bias_b_ref, first_b, sb_ref, eb_ref))
        oa_ref[pl.ds(r0, CHUNK), :] = oa.astype(oa_ref.dtype)
        ob_ref[pl.ds(r0, CHUNK), :] = ob.astype(ob_ref.dtype)
        return carry

    n_chunks = Q_BLOCK // CHUNK

    @pl.when(i == 0)
    def _():
        lax.fori_loop(0, n_chunks, functools.partial(chunk, masked=True), 0, unroll=2)

    @pl.when(i > 0)
    def _():
        lax.fori_loop(0, n_chunks, functools.partial(chunk, masked=False), 0, unroll=CHUNK_UNROLL)


def _attn_prompt(zr, kvb, kva, bias_a, bias_b, batch, seq):
    nblk = seq // Q_BLOCK
    blk = lambda col: pl.BlockSpec((Q_BLOCK, A_Q), lambda b, i: (b * nblk + i, col))
    return pl.pallas_call(
        _attn_prompt_kernel,
        grid=(batch, nblk),
        in_specs=[blk(0), blk(1), blk(0), blk(1), blk(0), blk(1),
                  _resident(bias_a.shape), _resident(bias_b.shape)],
        out_specs=[blk(0), blk(0)],
        out_shape=[jax.ShapeDtypeStruct((batch * seq, A_Q), BF16),
                   jax.ShapeDtypeStruct((batch * seq, B_W), BF16)],
        scratch_shapes=[pltpu.VMEM((2 * Q_BLOCK, A_Q), BF16) for _ in range(4)] + _score_scratch(),
        compiler_params=_params("arbitrary", "arbitrary"),
        name="attn_prompt",
    )(zr, zr, kvb, kvb, kva, kva, bias_a, bias_b)


def _attn_sample_kernel(qa_ref, qb_ref, kb_ref, vb_ref, ka_ref, va_ref,
                        ckb_ref, cvb_ref, cka_ref, cva_ref, sel_ref,
                        bias_a_ref, bias_b_ref, oa_ref, ob_ref,
                        kb_buf, vb_buf, ka_buf, va_buf, sa_ref, ea_ref, sb_ref, eb_ref):
    nb = ckb_ref.shape[0]
    kb_buf[:nb] = ckb_ref[...].astype(BF16)
    kb_buf[nb:] = kb_ref[...]
    vb_buf[:nb] = cvb_ref[...].astype(BF16)
    vb_buf[nb:] = vb_ref[...]
    na = cka_ref.shape[0]
    sel = sel_ref[...]
    ka_buf[:na] = jnp.dot(cka_ref[...].astype(BF16), sel, preferred_element_type=F32).astype(BF16)
    ka_buf[na:] = ka_ref[...]
    va_buf[:na] = jnp.dot(cva_ref[...].astype(BF16), sel, preferred_element_type=F32).astype(BF16)
    va_buf[na:] = va_ref[...]

    oa, ob = _chunk_attention(
        _mixer(qa_ref[...], ka_buf, va_buf, 0, A_BAND, bias_a_ref, None, sa_ref, ea_ref),
        _mixer(qb_ref[...], kb_buf, vb_buf, 0, B_BAND, bias_b_ref, None, sb_ref, eb_ref))
    oa_ref[...] = oa.astype(oa_ref.dtype)
    ob_ref[...] = ob.astype(ob_ref.dtype)


def _attn_sample(zr, kvb, kva, ckb, cvb, cka, cva, sel, bias_a, bias_b, row0, layer):
    nb = ckb.shape[1]
    c0 = row0 // CHUNK
    qblk = lambda col: pl.BlockSpec((CHUNK, A_Q), lambda b: (c0 + b, col))
    cache = lambda a: pl.BlockSpec((None, None) + a.shape[2:], lambda b: (layer, b, 0, 0))
    return pl.pallas_call(
        _attn_sample_kernel,
        grid=(nb,),
        in_specs=[qblk(0), qblk(1), qblk(0), qblk(1), qblk(0), qblk(1),
                  cache(ckb), cache(cvb), cache(cka), cache(cva), _resident(sel.shape),
                  _resident(bias_a.shape), _resident(bias_b.shape)],
        out_specs=[pl.BlockSpec((CHUNK, A_Q), lambda b: (b, 0)), pl.BlockSpec((CHUNK, B_W), lambda b: (b, 0))],
        out_shape=[jax.ShapeDtypeStruct((nb * CHUNK, A_Q), BF16),
                   jax.ShapeDtypeStruct((nb * CHUNK, B_W), BF16)],
        scratch_shapes=[pltpu.VMEM((B_BAND, B_W), BF16), pltpu.VMEM((B_BAND, B_W), BF16),
                        pltpu.VMEM((A_BAND, A_Q), BF16), pltpu.VMEM((A_BAND, A_Q), BF16)]
                       + _score_scratch(),
        compiler_params=_params("arbitrary"),
        name="attn_sample",
    )(zr, zr, kvb, kvb, kva, kva, ckb, cvb, cka, cva, sel, bias_a, bias_b)


def _merge_kernel(oap_ref, obp_ref, oas_ref, obs_ref, ga_ref, gb_ref, x_ref,
                  wa_ref, wb_ref, wo_ref, o_ref, *, n_p):
    def emit(oa_ref, ob_ref):
        ta = jnp.dot(oa_ref[...], wa_ref[...], preferred_element_type=F32)
        tb = jnp.dot(ob_ref[...], wb_ref[...], preferred_element_type=F32)
        mixed = (jax.nn.sigmoid(ga_ref[...].astype(F32)) * ta
                 + jax.nn.sigmoid(gb_ref[...].astype(F32)) * tb)
        o_ref[...] = x_ref[...] + jnp.dot(mixed.astype(BF16), wo_ref[...],
                                          preferred_element_type=F32)

    m = pl.program_id(0)
    pl.when(m < n_p)(lambda: emit(oap_ref, obp_ref))
    pl.when(m >= n_p)(lambda: emit(oas_ref, obs_ref))


def _merge(oa_p, ob_p, oa_s, ob_s, zr, x, wa, wb, wo, layer, tm=256):
    t = x.shape[0]
    n_p = oa_p.shape[0] // tm
    gate_blk = (A_Q + B_W) // D_MODEL
    prompt = pl.BlockSpec((tm, A_Q), lambda m: (_prompt_tile(m, n_p), 0))
    sample = pl.BlockSpec((tm, A_Q), lambda m: (_sample_tile(m, n_p), 0))
    return pl.pallas_call(
        functools.partial(_merge_kernel, n_p=n_p),
        grid=(t // tm,),
        in_specs=[prompt, prompt, sample, sample,
                  pl.BlockSpec((tm, D_MODEL), lambda m: (m, gate_blk)),
                  pl.BlockSpec((tm, D_MODEL), lambda m: (m, gate_blk + 1)),
                  pl.BlockSpec((tm, D_MODEL), lambda m: (m, 0)),
                  _layer_resident(wa, layer), _layer_resident(wb, layer), _layer_resident(wo, layer)],
        out_specs=pl.BlockSpec((tm, D_MODEL), lambda m: (m, 0)),
        out_shape=jax.ShapeDtypeStruct((t, D_MODEL), F32),
        compiler_params=_params("arbitrary"),
        name="merge",
    )(oa_p, ob_p, oa_s, ob_s, zr, zr, x, wa, wb, wo)


def _ffn_kernel(x_ref, g_ref, wg_ref, wu_ref, wd_ref, o_ref, h_ref):
    f = pl.program_id(1)

    @pl.when(f == 0)
    def _():
        x = x_ref[...]
        h_ref[...] = _rms(x, g_ref[...]).astype(BF16)
        o_ref[...] = x

    h = h_ref[...]
    gate = jnp.dot(h, wg_ref[...], preferred_element_type=F32)
    up = jnp.dot(h, wu_ref[...], preferred_element_type=F32)
    act = (jax.nn.silu(gate) * up).astype(BF16)
    o_ref[...] += jnp.dot(act, wd_ref[...], preferred_element_type=F32)


def _ffn(x, g, wg, wu, wd, layer, tm=1024):
    t = x.shape[0]
    _, n_f, _, tf = wg.shape
    return pl.pallas_call(
        _ffn_kernel,
        grid=(t // tm, n_f),
        in_specs=[pl.BlockSpec((tm, D_MODEL), lambda m, f: (m, 0)),
                  pl.BlockSpec((1, D_MODEL), lambda m, f: (0, 0)),
                  pl.BlockSpec((None, None, D_MODEL, tf), lambda m, f: (layer, f, 0, 0)),
                  pl.BlockSpec((None, None, D_MODEL, tf), lambda m, f: (layer, f, 0, 0)),
                  pl.BlockSpec((None, tf, D_MODEL), lambda m, f: (layer, f, 0))],
        out_specs=pl.BlockSpec((tm, D_MODEL), lambda m, f: (m, 0)),
        out_shape=jax.ShapeDtypeStruct((t, D_MODEL), F32),
        scratch_shapes=[pltpu.VMEM((tm, D_MODEL), BF16)],
        compiler_params=_params("arbitrary", "arbitrary"),
        name="ffn",
    )(x, g.reshape(1, D_MODEL), wg, wu, wd)


def _ple_update(x_ref, pp_ref, ps_ref, wpg_ref, wp_ref, n_p):
    x = x_ref[...]
    p = jnp.where(pl.program_id(0) < n_p, pp_ref[...], ps_ref[...])
    gate = jax.nn.sigmoid(jnp.dot(x.astype(BF16), wpg_ref[...], preferred_element_type=F32))
    emb = jnp.dot(p.astype(BF16), wp_ref[...], preferred_element_type=F32)
    return x + gate * emb


def _ple_kernel(x_ref, pp_ref, ps_ref, wpg_ref, wp_ref, g_ref, o_ref, h_ref, *, n_p):
    y = _ple_update(x_ref, pp_ref, ps_ref, wpg_ref, wp_ref, n_p)
    o_ref[...] = y
    h_ref[...] = _rms(y, g_ref[...]).astype(h_ref.dtype)


def _ple_final_kernel(x_ref, pp_ref, ps_ref, wpg_ref, wp_ref, g_ref, yp_ref, ys_ref, *, n_p):
    y = _rms(_ple_update(x_ref, pp_ref, ps_ref, wpg_ref, wp_ref, n_p), g_ref[...])
    m = pl.program_id(0)

    @pl.when(m < n_p)
    def _():
        yp_ref[...] = y

    @pl.when(m >= n_p)
    def _():
        ys_ref[...] = y


def _ple(x, pp, ps, layer, wpg, wp, g_next, final, tm=512):
    t = x.shape[0]
    tp, ts = pp.shape[1], ps.shape[1]
    n_p = tp // tm
    row = lambda m: (m, 0)
    in_specs = [pl.BlockSpec((tm, D_MODEL), row),
                pl.BlockSpec((None, tm, PLE_DIM), lambda m: (layer, _prompt_tile(m, n_p), 0)),
                pl.BlockSpec((None, tm, PLE_DIM), lambda m: (layer, _sample_tile(m, n_p), 0)),
                _layer_resident(wpg, layer), _layer_resident(wp, layer),
                pl.BlockSpec((1, D_MODEL), lambda m: (0, 0))]
    if final:
        body = _ple_final_kernel
        out_specs = [pl.BlockSpec((tm, D_MODEL), lambda m: (_prompt_tile(m, n_p), 0)),
                     pl.BlockSpec((tm, D_MODEL), lambda m: (_sample_tile(m, n_p), 0))]
        out_shape = [jax.ShapeDtypeStruct((tp, D_MODEL), F32), jax.ShapeDtypeStruct((ts, D_MODEL), F32)]
    else:
        body = _ple_kernel
        out_specs = [pl.BlockSpec((tm, D_MODEL), row), pl.BlockSpec((tm, D_MODEL), row)]
        out_shape = [jax.ShapeDtypeStruct((t, D_MODEL), F32), jax.ShapeDtypeStruct((t, D_MODEL), BF16)]
    return pl.pallas_call(
        functools.partial(body, n_p=n_p),
        grid=(t // tm,),
        in_specs=in_specs,
        out_specs=out_specs,
        out_shape=out_shape,
        compiler_params=_params("arbitrary"),
        name="ple",
    )(x, pp, ps, wpg, wp, g_next.reshape(1, D_MODEL))


def kernel(x_prompt, x_sample, cache_a_k, cache_a_v, cache_b_k, cache_b_v, p_prompt, p_sample,
           attn_norm, w_in, a_sink, t5_table, b_rel_table, w_a_out, w_b_out, w_o, ffn_norm,
           w_gate, w_up, w_down, w_ple, w_ple_gate, final_norm):
    depth = w_in.shape[0]
    batch, seq, _ = x_prompt.shape
    dec_batch, dec_seq, _ = x_sample.shape
    tp = batch * seq
    ts = dec_batch * dec_seq
    assert dec_seq == CHUNK and seq % Q_BLOCK == 0
    assert cache_a_k.shape[2] == A_BACK_CHUNKS * CHUNK and cache_b_k.shape[2] == B_BACK_CHUNKS * CHUNK
    a_keep = min(A_BACK_CHUNKS * CHUNK, seq)
    b_keep = min(B_BACK_CHUNKS * CHUNK, seq)

    src = jnp.arange(A_KV, dtype=jnp.int32)
    dst = jnp.arange(A_Q, dtype=jnp.int32)
    sel = ((src[:, None] // HEAD_DIM == dst[None, :] // GROUP_W)
           & (src[:, None] % HEAD_DIM == dst[None, :] % HEAD_DIM)).astype(BF16)

    bias_a = _bias_lookup(_t5_bucket(_band_rel(A_BAND)), t5_table, A_BAND)
    idx_b = jnp.clip(-_band_rel(B_BAND), -B_REL_CLIP, B_REL_CLIP) + B_REL_CLIP

    pp = p_prompt.reshape(depth, tp, PLE_DIM)
    ps = p_sample.reshape(depth, ts, PLE_DIM)
    ckb = cache_b_k.reshape(depth, dec_batch, -1, B_W)
    cvb = cache_b_v.reshape(depth, dec_batch, -1, B_W)
    cka = cache_a_k.reshape(depth, dec_batch, -1, A_KV)
    cva = cache_a_v.reshape(depth, dec_batch, -1, A_KV)

    w_kv, w_rest = _split_w_in(w_in)
    wg, wu = _ffn_tiles(w_gate), _ffn_tiles(w_up)
    wd, wa, wb, wo = (w.astype(BF16) for w in (w_down, w_a_out, w_b_out, w_o))
    wpg, wp = w_ple_gate.astype(BF16), w_ple.astype(BF16)

    x, h = _norm0(x_prompt.reshape(tp, D_MODEL), x_sample.reshape(ts, D_MODEL), attn_norm[0])
    caches = ([], [], [], [], [], [], [], [])
    for i in range(depth):
        kvb, kva, kbp, vbp, kap, vap, kbs, vbs, kas, vas = _kv_proj(
            h, w_kv, i, sel, batch, seq, ts, a_keep, b_keep)
        zr = _rest_proj(h, w_rest, i)

        bias_b = _bias_lookup(idx_b, b_rel_table[i], B_BAND)
        sink = jnp.repeat(a_sink[i], CHUNK).reshape(N_GROUPS, GROUP_ROWS, 1)
        pad = jnp.full((N_GROUPS, GROUP_ROWS, A_WIDTH - A_BAND - 1), NEG_INF, F32)
        bias_a_l = jnp.concatenate([bias_a, sink, pad], axis=-1)

        oa_p, ob_p = _attn_prompt(zr, kvb, kva, bias_a_l, bias_b, batch, seq)
        oa_s, ob_s = _attn_sample(zr, kvb, kva, ckb, cvb, cka, cva, sel, bias_a_l, bias_b, tp, i)

        x = _merge(oa_p, ob_p, oa_s, ob_s, zr, x, wa, wb, wo, i)
        x = _ffn(x, ffn_norm[i], wg, wu, wd, i)
        last = i == depth - 1
        g_next = final_norm if last else attn_norm[i + 1]
        x, h = _ple(x, pp, ps, i, wpg, wp, g_next, last)

        pieces = (
            kap.reshape(batch, a_keep, A_KV_HEADS, HEAD_DIM),
            vap.reshape(batch, a_keep, A_KV_HEADS, HEAD_DIM),
            kbp.reshape(batch, b_keep, B_HEADS, HEAD_DIM),
            vbp.reshape(batch, b_keep, B_HEADS, HEAD_DIM),
            kas.reshape(dec_batch, dec_seq, A_KV_HEADS, HEAD_DIM),
            vas.reshape(dec_batch, dec_seq, A_KV_HEADS, HEAD_DIM),
            kbs.reshape(dec_batch, dec_seq, B_HEADS, HEAD_DIM),
            vbs.reshape(dec_batch, dec_seq, B_HEADS, HEAD_DIM),
        )
        for lst, piece in zip(caches, pieces):
            lst.append(piece)

    y_p, y_s = x, h
    return (y_p.reshape(batch, seq, D_MODEL), y_s.reshape(dec_batch, dec_seq, D_MODEL),
            *[jnp.stack(lst, 0) for lst in caches])
```
